```python
import jax, jax.numpy as jnp
from jax import lax
import numpy as np

D_MODEL = 1024
BATCH = 8
SEQ = 2048
DEPTH = 2
DEC_BATCH = 16
DEC_SEQ = 32
PAST_LEN = 1024

CHUNK = 64
CONV_W = 4
N_A = (DEPTH + 1) // 2
N_C = DEPTH // 2
NORM_EPS = 1e-6
LRU_WIDTH = D_MODEL // 2
LRU_HEADS = 8
LRU_BLOCK = LRU_WIDTH // LRU_HEADS
LRU_C = 8.0
RET_HEADS = 4
RET_DK = D_MODEL // 8
RET_DV = D_MODEL // 8
ROPE_BASE = 10000.0
GLA_HEADS = 4
GLA_DK = D_MODEL // 16
GLA_DV = D_MODEL // 8
GLA_RANK = 16
GLA_TAU = 16.0
MLSTM_HEADS = 4
MLSTM_DH = D_MODEL // 8
MLSTM_WIDTH = MLSTM_HEADS * MLSTM_DH
D_FF = -(-8 * D_MODEL // (3 * 256)) * 256
EVEN_SPLITS = [LRU_WIDTH, LRU_WIDTH, RET_HEADS * RET_DK, RET_HEADS * RET_DK,
               RET_HEADS * RET_DV, RET_HEADS * RET_DV]
EVEN_IN = sum(EVEN_SPLITS)
EVEN_OUT = LRU_WIDTH + RET_HEADS * RET_DV
ODD_SPLITS = [GLA_HEADS * GLA_DK, GLA_HEADS * GLA_DK, GLA_HEADS * GLA_DV, GLA_RANK,
              GLA_HEADS * GLA_DV, MLSTM_WIDTH, MLSTM_WIDTH, MLSTM_HEADS, MLSTM_HEADS]
ODD_IN = sum(ODD_SPLITS)
ODD_OUT = GLA_HEADS * GLA_DV + MLSTM_WIDTH
STATE_KEYS = ('lru_h', 'lru_conv', 'ret', 'gla', 'mlstm_C', 'mlstm_n', 'mlstm_m', 'mlstm_conv')

kernel_name = 'hybrid_streaming_encoder_step'


def _rms_norm(x, g):
    xf = x.astype(jnp.float32)
    y = xf * lax.rsqrt(jnp.mean(xf * xf, axis=-1, keepdims=True) + NORM_EPS)
    return (y * g.astype(jnp.float32)).astype(x.dtype)


def _head_norm(x, g):
    xf = x.astype(jnp.float32)
    xc = xf - jnp.mean(xf, axis=-1, keepdims=True)
    y = xc * lax.rsqrt(jnp.mean(xc * xc, axis=-1, keepdims=True) + NORM_EPS)
    return y.reshape(x.shape[:2] + (-1,)) * g.astype(jnp.float32)


def _split(z, sizes):
    return jnp.split(z, np.cumsum(sizes)[:-1].tolist(), axis=-1)


def _to_chunks(t, L):
    b, T = t.shape[0], t.shape[1]
    return jnp.swapaxes(t.reshape((b, T // L, L) + t.shape[2:]), 0, 1)


def _from_chunks(t):
    t = jnp.swapaxes(t, 0, 1)
    return t.reshape((t.shape[0], t.shape[1] * t.shape[2]) + t.shape[3:])


def _causal_dwconv(u, buf, w, b):
    T = u.shape[1]
    up = jnp.concatenate([buf.astype(u.dtype), u], axis=1)
    out = b + w[0] * up[:, 0:T]
    for tap in range(1, CONV_W):
        out = out + w[tap] * up[:, tap:tap + T]
    return out, up[:, T:]


def _rotary(x, pos):
    half = x.shape[-1] // 2
    inv = ROPE_BASE ** (-jnp.arange(half, dtype=jnp.float32) / half)
    ang = pos.astype(jnp.float32)[:, None] * inv[None, :]
    cos = jnp.cos(ang)[None, :, None, :]
    sin = jnp.sin(ang)[None, :, None, :]
    x1, x2 = x[..., :half], x[..., half:]
    return jnp.concatenate([x1 * cos - x2 * sin, x1 * sin + x2 * cos], axis=-1)


def _rglru(xc, h0, gate_w, gate_b, lam):
    B, T, W = xc.shape
    gates = jnp.einsum('bthi,hij->bthj', xc.reshape(B, T, LRU_HEADS, LRU_BLOCK), gate_w) + gate_b
    r = jax.nn.sigmoid(gates[..., :LRU_BLOCK]).reshape(B, T, W)
    i = jax.nn.sigmoid(gates[..., LRU_BLOCK:]).reshape(B, T, W)
    log_a = -LRU_C * r * jax.nn.softplus(-lam)
    a = jnp.exp(log_a)
    b = jnp.sqrt(-jnp.expm1(2.0 * log_a)) * (i * xc)
    b = b.at[:, 0].add(a[:, 0] * h0)

    def combine(left, right):
        return left[0] * right[0], right[0] * left[1] + right[1]

    _, h = lax.associative_scan(combine, (a, b), axis=1)
    return h, h[:, -1]


def _retention(q, k, v, S0):
    B, T, H, _ = q.shape
    L = min(CHUNK, T)
    log_g = jnp.log1p(-jnp.exp2(-5.0 - jnp.arange(H, dtype=jnp.float32)))
    idx = jnp.arange(L, dtype=jnp.float32)
    diff = idx[:, None] - idx[None, :]
    decay = jnp.where(diff >= 0, jnp.exp(jnp.maximum(diff, 0.0)[None] * log_g[:, None, None]), 0.0)
    q_dec = jnp.exp((idx[:, None] + 1.0) * log_g[None, :])
    k_dec = jnp.exp((L - 1.0 - idx)[:, None] * log_g[None, :])
    s_dec = jnp.exp(L * log_g)

    def step(S, inp):
        qc, kc, vc = inp
        att = jnp.einsum('blhd,bshd->bhls', qc, kc) * decay
        o = (jnp.einsum('bhls,bshe->blhe', att, vc)
             + jnp.einsum('blhd,bhde->blhe', qc, S) * q_dec[None, :, :, None])
        S = s_dec[None, :, None, None] * S + jnp.einsum('bshd,bshe->bhde', kc * k_dec[None, :, :, None], vc)
        return S, o

    S, o = lax.scan(step, S0.astype(jnp.float32), (_to_chunks(q, L), _to_chunks(k, L), _to_chunks(v, L)))
    return _from_chunks(o), S


def _gla(q, k, v, log_alpha, S0):
    B, T, H, _ = q.shape
    L = min(CHUNK, T)
    causal = jnp.tril(jnp.ones((L, L), dtype=bool))

    def step(S, inp):
        qc, kc, vc, lac = inp
        cum = jnp.cumsum(lac, axis=1)
        last = cum[:, -1:]
        ref = 0.5 * last
        att = jnp.einsum('blhd,bshd->bhls', qc * jnp.exp(cum - ref), kc * jnp.exp(ref - cum))
        att = jnp.where(causal, att, 0.0)
        o = (jnp.einsum('bhls,bshe->blhe', att, vc)
             + jnp.einsum('blhd,bhde->blhe', qc * jnp.exp(cum), S))
        S = jnp.exp(last[:, 0])[..., None] * S + jnp.einsum('bshd,bshe->bhde', kc * jnp.exp(last - cum), vc)
        return S, o

    S, o = lax.scan(step, S0.astype(jnp.float32),
                    (_to_chunks(q, L), _to_chunks(k, L), _to_chunks(v, L), _to_chunks(log_alpha, L)))
    return _from_chunks(o), S


def _mlstm(q, k, v, log_i, log_f, C0, n0, m0):
    B, T, H, _ = q.shape
    L = min(CHUNK, T)
    causal = jnp.tril(jnp.ones((L, L), dtype=bool))

    def step(carry, inp):
        C, n, m = carry
        qc, kc, vc, ic, fc = inp
        b = jnp.cumsum(fc, axis=1)
        a = b + m[:, None, :]
        dlog = b[:, :, None, :] - b[:, None, :, :] + ic[:, None, :, :]
        dlog = jnp.where(causal[None, :, :, None], dlog, -jnp.inf)
        m_t = jnp.maximum(a, jnp.max(dlog, axis=2))
        w = jnp.exp(dlog - m_t[:, :, None, :])
        w_in = jnp.exp(a - m_t)
        qk = jnp.einsum('blhd,bshd->blsh', qc, kc) * w
        num = (jnp.einsum('blsh,bshe->blhe', qk, vc)
               + w_in[..., None] * jnp.einsum('blhd,bhde->blhe', qc, C))
        den = jnp.sum(qk, axis=2) + w_in * jnp.einsum('blhd,bhd->blh', qc, n)
        hc = num / jnp.maximum(jnp.abs(den), jnp.exp(-m_t))[..., None]
        w_last = w[:, -1]
        C = w_in[:, -1, :, None, None] * C + jnp.einsum('bsh,bshd,bshe->bhde', w_last, kc, vc)
        n = w_in[:, -1, :, None] * n + jnp.einsum('bsh,bshd->bhd', w_last, kc)
        return (C, n, m_t[:, -1]), hc

    xs = tuple(_to_chunks(t, L) for t in (q, k, v, log_i, log_f))
    (C, n, m), h = lax.scan(step, (C0.astype(jnp.float32), n0.astype(jnp.float32), m0.astype(jnp.float32)), xs)
    return _from_chunks(h), C, n, m


def _even_mixer(h, pos, P, j, h0, conv0, S0):
    B, T, _ = h.shape
    z = (h @ P['a_w_in'][j]).astype(jnp.float32)
    xb, gb, q, k, v, g = _split(z, EVEN_SPLITS)
    xc, conv_new = _causal_dwconv(xb, conv0, P['lru_conv_w'][j], P['lru_conv_b'][j])
    hs, h_last = _rglru(xc, h0.astype(jnp.float32), P['lru_gate_w'][j], P['lru_gate_b'][j], P['lru_lambda'][j])
    out_a = jax.nn.gelu(gb) * hs
    q = _rotary(q.reshape(B, T, RET_HEADS, RET_DK), pos)
    k = _rotary(k.reshape(B, T, RET_HEADS, RET_DK), pos) * (RET_DK ** -0.5)
    v = v.reshape(B, T, RET_HEADS, RET_DV)
    o, S_new = _retention(q, k, v, S0)
    out_b = jax.nn.silu(g) * _head_norm(o, P['ret_gn_g'][j])
    y = jnp.concatenate([out_a, out_b], axis=-1).astype(h.dtype) @ P['a_w_out'][j]
    return y, h_last, conv_new, S_new


def _odd_mixer(h, P, j, S0, C0, n0, m0, conv0):
    B, T, _ = h.shape
    z = (h @ P['c_w_in'][j]).astype(jnp.float32)
    gq, gk, gv, glr, gr, u, o_pre, i_pre, f_pre = _split(z, ODD_SPLITS)
    log_alpha = jax.nn.log_sigmoid(glr @ P['gla_w_gate'][j] + P['gla_b_gate'][j]) / GLA_TAU
    o_c, S_new = _gla(gq.reshape(B, T, GLA_HEADS, GLA_DK),
                      gk.reshape(B, T, GLA_HEADS, GLA_DK) * (GLA_DK ** -0.5),
                      gv.reshape(B, T, GLA_HEADS, GLA_DV),
                      log_alpha.reshape(B, T, GLA_HEADS, GLA_DK), S0)
    out_c = jax.nn.silu(gr) * _head_norm(o_c, P['gla_gn_g'][j])
    uc, conv_new = _causal_dwconv(u, conv0, P['mlstm_conv_w'][j], P['mlstm_conv_b'][j])
    uc = jax.nn.silu(uc)
    qk = jnp.einsum('bthi,hij->bthj', uc.reshape(B, T, MLSTM_HEADS, MLSTM_DH), P['mlstm_w_qk'][j])
    mv = jnp.einsum('bthi,hij->bthj', u.reshape(B, T, MLSTM_HEADS, MLSTM_DH), P['mlstm_w_v'][j])
    b_if = P['mlstm_b_if'][j]
    hm, C_new, n_new, m_new = _mlstm(qk[..., :MLSTM_DH], qk[..., MLSTM_DH:] * (MLSTM_DH ** -0.5), mv,
                                     i_pre + b_if[:MLSTM_HEADS],
                                     jax.nn.log_sigmoid(f_pre + b_if[MLSTM_HEADS:]), C0, n0, m0)
    out_d = jax.nn.sigmoid(o_pre) * _head_norm(hm, P['mlstm_gn_g'][j])
    y = jnp.concatenate([out_c, out_d], axis=-1).astype(h.dtype) @ P['c_w_out'][j]
    return y, S_new, C_new, n_new, m_new, conv_new


def _swiglu(h, w_in, w_out):
    gu = h @ w_in
    return (jax.nn.silu(gu[..., :D_FF]) * gu[..., D_FF:]) @ w_out


def _trunk(x, c, st, P, pos0):
    T = x.shape[1]
    pos = pos0 + jnp.arange(T, dtype=jnp.int32)
    cs = jax.nn.silu(c)
    new = {name: [] for name in STATE_KEYS}
    for l in range(DEPTH):
        mod = cs @ P['w_mod'][l] + P['b_mod'][l]
        sh1, sc1, g1, sh2, sc2, g2 = jnp.split(mod[:, None, :], 6, axis=-1)
        hn = _rms_norm(x, P['norm_mix_g'][l]) * (1.0 + sc1) + sh1
        j = l // 2
        if l % 2 == 0:
            y, h_new, lconv_new, ret_new = _even_mixer(hn, pos, P, j, st['lru_h'][j], st['lru_conv'][j], st['ret'][j])
            new['lru_h'].append(h_new)
            new['lru_conv'].append(lconv_new)
            new['ret'].append(ret_new)
        else:
            y, gla_new, C_new, n_new, m_new, mconv_new = _odd_mixer(
                hn, P, j, st['gla'][j], st['mlstm_C'][j], st['mlstm_n'][j], st['mlstm_m'][j], st['mlstm_conv'][j])
            new['gla'].append(gla_new)
            new['mlstm_C'].append(C_new)
            new['mlstm_n'].append(n_new)
            new['mlstm_m'].append(m_new)
            new['mlstm_conv'].append(mconv_new)
        x = x + g1 * y
        hn = _rms_norm(x, P['norm_ffn_g'][l]) * (1.0 + sc2) + sh2
        x = x + g2 * _swiglu(hn, P['w_ffn_in'][l], P['w_ffn_out'][l])
    y = _rms_norm(x, P['final_norm_g'])
    return y, {name: jnp.stack(vals) for name, vals in new.items()}


def _zero_states(nb):
    f32 = jnp.float32
    return {
        'lru_h': jnp.zeros((N_A, nb, LRU_WIDTH), f32),
        'lru_conv': jnp.zeros((N_A, nb, CONV_W - 1, LRU_WIDTH), f32),
        'ret': jnp.zeros((N_A, nb, RET_HEADS, RET_DK, RET_DV), f32),
        'gla': jnp.zeros((N_C, nb, GLA_HEADS, GLA_DK, GLA_DV), f32),
        'mlstm_C': jnp.zeros((N_C, nb, MLSTM_HEADS, MLSTM_DH, MLSTM_DH), f32),
        'mlstm_n': jnp.zeros((N_C, nb, MLSTM_HEADS, MLSTM_DH), f32),
        'mlstm_m': jnp.zeros((N_C, nb, MLSTM_HEADS), f32),
        'mlstm_conv': jnp.zeros((N_C, nb, CONV_W - 1, MLSTM_WIDTH), f32),
    }


def setup_inputs(seed: int = 0) -> dict:
    key = jax.random.key(seed)
    keys = iter(jax.random.split(key, 64))
    f32 = jnp.float32

    def nrm(shape, scale):
        return jax.random.normal(next(keys), shape, f32) * scale

    def gain(shape):
        return 1.0 + nrm(shape, 0.02)

    u = jax.random.uniform(next(keys), (N_A, LRU_WIDTH), f32, 0.9, 0.999)
    a = u ** (1.0 / LRU_C)
    lru_lambda = jnp.log(a) - jnp.log1p(-a)
    b_forget = jnp.linspace(3.0, 6.0, MLSTM_HEADS, dtype=f32)[None, :] + nrm((N_C, MLSTM_HEADS), 0.1)
    mlstm_b_if = jnp.concatenate([nrm((N_C, MLSTM_HEADS), 0.1), b_forget], axis=-1)
    return {
        'x_prompt': nrm((BATCH, SEQ, D_MODEL), 1.0),
        'x_sample': nrm((DEC_BATCH, DEC_SEQ, D_MODEL), 1.0),
        'c_prompt': nrm((BATCH, D_MODEL), 1.0),
        'c_sample': nrm((DEC_BATCH, D_MODEL), 1.0),
        'state_lru_h': nrm((N_A, DEC_BATCH, LRU_WIDTH), 0.5),
        'state_lru_conv': nrm((N_A, DEC_BATCH, CONV_W - 1, LRU_WIDTH), 1.0),
        'state_ret': nrm((N_A, DEC_BATCH, RET_HEADS, RET_DK, RET_DV), 1.0),
        'state_gla': nrm((N_C, DEC_BATCH, GLA_HEADS, GLA_DK, GLA_DV), 1.0),
        'state_mlstm_C': nrm((N_C, DEC_BATCH, MLSTM_HEADS, MLSTM_DH, MLSTM_DH), 0.5),
        'state_mlstm_n': nrm((N_C, DEC_BATCH, MLSTM_HEADS, MLSTM_DH), 0.5),
        'state_mlstm_m': nrm((N_C, DEC_BATCH, MLSTM_HEADS), 1.0),
        'state_mlstm_conv': nrm((N_C, DEC_BATCH, CONV_W - 1, MLSTM_WIDTH), 1.0),
        'w_mod': nrm((DEPTH, D_MODEL, 6 * D_MODEL), 0.5 * D_MODEL ** -0.5),
        'b_mod': nrm((DEPTH, 6 * D_MODEL), 0.02),
        'norm_mix_g': gain((DEPTH, D_MODEL)),
        'norm_ffn_g': gain((DEPTH, D_MODEL)),
        'w_ffn_in': nrm((DEPTH, D_MODEL, 2 * D_FF), D_MODEL ** -0.5),
        'w_ffn_out': nrm((DEPTH, D_FF, D_MODEL), D_FF ** -0.5),
        'final_norm_g': gain((D_MODEL,)),
        'a_w_in': nrm((N_A, D_MODEL, EVEN_IN), D_MODEL ** -0.5),
        'a_w_out': nrm((N_A, EVEN_OUT, D_MODEL), EVEN_OUT ** -0.5),
        'lru_conv_w': nrm((N_A, CONV_W, LRU_WIDTH), CONV_W ** -0.5),
        'lru_conv_b': nrm((N_A, LRU_WIDTH), 0.02),
        'lru_gate_w': nrm((N_A, LRU_HEADS, LRU_BLOCK, 2 * LRU_BLOCK), LRU_BLOCK ** -0.5),
        'lru_gate_b': nrm((N_A, LRU_HEADS, 2 * LRU_BLOCK), 0.02),
        'lru_lambda': lru_lambda,
        'ret_gn_g': gain((N_A, RET_HEADS * RET_DV)),
        'c_w_in': nrm((N_C, D_MODEL, ODD_IN), D_MODEL ** -0.5),
        'c_w_out': nrm((N_C, ODD_OUT, D_MODEL), ODD_OUT ** -0.5),
        'gla_w_gate': nrm((N_C, GLA_RANK, GLA_HEADS * GLA_DK), GLA_RANK ** -0.5),
        'gla_b_gate': nrm((N_C, GLA_HEADS * GLA_DK), 0.02),
        'gla_gn_g': gain((N_C, GLA_HEADS * GLA_DV)),
        'mlstm_conv_w': nrm((N_C, CONV_W, MLSTM_WIDTH), CONV_W ** -0.5),
        'mlstm_conv_b': nrm((N_C, MLSTM_WIDTH), 0.02),
        'mlstm_w_qk': nrm((N_C, MLSTM_HEADS, MLSTM_DH, 2 * MLSTM_DH), MLSTM_DH ** -0.5),
        'mlstm_w_v': nrm((N_C, MLSTM_HEADS, MLSTM_DH, MLSTM_DH), MLSTM_DH ** -0.5),
        'mlstm_b_if': mlstm_b_if,
        'mlstm_gn_g': gain((N_C, MLSTM_WIDTH)),
    }


def reference(x_prompt, x_sample, c_prompt, c_sample,
              state_lru_h, state_lru_conv, state_ret, state_gla,
              state_mlstm_C, state_mlstm_n, state_mlstm_m, state_mlstm_conv,
              w_mod, b_mod, norm_mix_g, norm_ffn_g, w_ffn_in, w_ffn_out, final_norm_g,
              a_w_in, a_w_out, lru_conv_w, lru_conv_b, lru_gate_w, lru_gate_b, lru_lambda, ret_gn_g,
              c_w_in, c_w_out, gla_w_gate, gla_b_gate, gla_gn_g,
              mlstm_conv_w, mlstm_conv_b, mlstm_w_qk, mlstm_w_v, mlstm_b_if, mlstm_gn_g):
    P = {
        'w_mod': w_mod, 'b_mod': b_mod, 'norm_mix_g': norm_mix_g, 'norm_ffn_g': norm_ffn_g,
        'w_ffn_in': w_ffn_in, 'w_ffn_out': w_ffn_out, 'final_norm_g': final_norm_g,
        'a_w_in': a_w_in, 'a_w_out': a_w_out, 'lru_conv_w': lru_conv_w, 'lru_conv_b': lru_conv_b,
        'lru_gate_w': lru_gate_w, 'lru_gate_b': lru_gate_b, 'lru_lambda': lru_lambda, 'ret_gn_g': ret_gn_g,
        'c_w_in': c_w_in, 'c_w_out': c_w_out, 'gla_w_gate': gla_w_gate, 'gla_b_gate': gla_b_gate,
        'gla_gn_g': gla_gn_g, 'mlstm_conv_w': mlstm_conv_w, 'mlstm_conv_b': mlstm_conv_b,
        'mlstm_w_qk': mlstm_w_qk, 'mlstm_w_v': mlstm_w_v, 'mlstm_b_if': mlstm_b_if, 'mlstm_gn_g': mlstm_gn_g,
    }
    y_prompt, sp = _trunk(x_prompt, c_prompt, _zero_states(x_prompt.shape[0]), P, 0)
    past = {
        'lru_h': state_lru_h, 'lru_conv': state_lru_conv, 'ret': state_ret, 'gla': state_gla,
        'mlstm_C': state_mlstm_C, 'mlstm_n': state_mlstm_n, 'mlstm_m': state_mlstm_m,
        'mlstm_conv': state_mlstm_conv,
    }
    y_sample, ss = _trunk(x_sample, c_sample, past, P, PAST_LEN)
    return (y_prompt, y_sample,
            sp['lru_h'], sp['lru_conv'], sp['ret'], sp['gla'],
            sp['mlstm_C'], sp['mlstm_n'], sp['mlstm_m'], sp['mlstm_conv'],
            ss['lru_h'], ss['lru_conv'], ss['ret'], ss['gla'],
            ss['mlstm_C'], ss['mlstm_n'], ss['mlstm_m'], ss['mlstm_conv'])
```

```python
import functools

import numpy as np
import jax
import jax.numpy as jnp
from jax import lax
from jax.experimental import pallas as pl
from jax.experimental.pallas import tpu as pltpu

F32 = jnp.float32
BF16 = jnp.bfloat16

D_MODEL = 1024
PAST_LEN = 1024
CHUNK = 64
CONV_W = 4
NORM_EPS = 1e-6
LRU_WIDTH = 512
LRU_HEADS = 8
LRU_BLOCK = 64
LRU_C = 8.0
RET_HEADS = 4
RET_D = 128
ROPE_BASE = 10000.0
GLA_HEADS = 4
GLA_DK = 64
GLA_DV = 128
GLA_RANK = 16
GLA_TAU = 16.0
MLSTM_HEADS = 4
MLSTM_DH = 128
D_FF = 2816
EVEN_IN = 3072
ODD_IN_PADDED = 2688
FFN_COLS = 256

VMEM_LIMIT_BYTES = 56 * 1024 * 1024
CONV_PAD = 8


def _dot(a, b):
    return jnp.dot(a.astype(BF16), b.astype(BF16), preferred_element_type=F32)


def _dot_nt(a, b):
    return lax.dot_general(a.astype(BF16), b.astype(BF16), (((1,), (1,)), ((), ())),
                           preferred_element_type=F32)


def _dot_tn(a, b):
    return lax.dot_general(a.astype(BF16), b.astype(BF16), (((0,), (0,)), ((), ())),
                           preferred_element_type=F32)


def _sigmoid(x):
    return jax.nn.sigmoid(x)


def _silu(x):
    return x * jax.nn.sigmoid(x)


def _softplus(x):
    return jnp.maximum(x, 0.0) + jnp.log1p(jnp.exp(-jnp.abs(x)))


def _log_sigmoid(x):
    return -_softplus(-x)


def _gelu_tanh(x):
    c = np.sqrt(2.0 / np.pi).astype(np.float32)
    return 0.5 * x * (1.0 + jnp.tanh(c * (x + 0.044715 * (x * x * x))))


def _rms_mod(x, gain, scale, shift):
    y = x * lax.rsqrt(jnp.mean(x * x, axis=-1, keepdims=True) + NORM_EPS)
    return (y * gain[None]) * (1.0 + scale) + shift


def _head_norm(o, gain):
    oc = o - jnp.mean(o, axis=-1, keepdims=True)
    y = oc * lax.rsqrt(jnp.mean(oc * oc, axis=-1, keepdims=True) + NORM_EPS)
    return y * gain


def _row_pos(shape, period):
    return jnp.bitwise_and(lax.broadcasted_iota(jnp.int32, shape, 0), period - 1)


def _seg_cumsum(x, pos, length):
    d = 1
    while d < length:
        x = x + jnp.where(pos >= d, pltpu.roll(x, d, 0), 0.0)
        d *= 2
    return x


def _seg_linear_scan(a, b, pos, length):
    d = 1
    while d < length:
        valid = pos >= d
        b = b + jnp.where(valid, a * pltpu.roll(b, d, 0), 0.0)
        if 2 * d < length:
            a = a * jnp.where(valid, pltpu.roll(a, d, 0), 1.0)
        d *= 2
    return b


def _conv_taps(cbuf, xin, w_ref, b_ref, nb, tt):
    width = xin.shape[-1]
    cbuf[:, CONV_PAD:CONV_PAD + tt, :] = xin.reshape(nb, tt, width)
    base = CONV_PAD - (CONV_W - 1)
    out = b_ref[...][None] + w_ref[0:1, :][None] * cbuf[:, base:base + tt, :]
    for tap in range(1, CONV_W):
        out = out + w_ref[tap:tap + 1, :][None] * cbuf[:, base + tap:base + tap + tt, :]
    tail = cbuf[:, base + tt:CONV_PAD + tt, :]
    cbuf[:, base:CONV_PAD, :] = tail
    return out.reshape(nb * tt, width), tail


def _mod_kernel(c_ref, w_ref, b_ref, o_ref):
    c = c_ref[...]
    o_ref[0] = _dot(_silu(c), w_ref[0]) + b_ref[0]


def _modulation(c_all, w_mod, b_mod):
    depth, d, n = w_mod.shape
    rows = c_all.shape[0]
    tn = 1024
    return pl.pallas_call(
        _mod_kernel,
        grid=(depth, n // tn),
        in_specs=[
            pl.BlockSpec((rows, d), lambda l, j: (0, 0)),
            pl.BlockSpec((1, d, tn), lambda l, j: (l, 0, j)),
            pl.BlockSpec((1, 1, tn), lambda l, j: (l, 0, j)),
        ],
        out_specs=pl.BlockSpec((1, rows, tn), lambda l, j: (l, 0, j)),
        out_shape=jax.ShapeDtypeStruct((depth, rows, n), F32),
        compiler_params=pltpu.CompilerParams(
            dimension_semantics=("arbitrary", "arbitrary"), vmem_limit_bytes=VMEM_LIMIT_BYTES),
        name="adaln_modulation",
    )(c_all, w_mod, b_mod.reshape(depth, 1, n))


def _even_kernel(x_ref, mod_ref, ng_ref, win_ref, wout_ref, cw_ref, cb_ref, wg_ref, gb_ref, lam_ref,
                 gn_ref, cos_ref, sin_ref, dec_ref, qdec_ref, kdec_ref, h0_ref, c0_ref, s0_ref,
                 y_ref, h_ref, c_ref, s_ref, z_scr, cbuf, mix_scr, *, nb, tt, chunk, sdec):
    d = D_MODEL
    m = nb * tt
    w = LRU_WIDTH
    q0, k0, v0, g0 = 2 * w, 3 * w, 4 * w, 5 * w

    @pl.when(pl.program_id(1) == 0)
    def _():
        h_ref[...] = h0_ref[...]
        s_ref[...] = s0_ref[...]
        cbuf[:, CONV_PAD - (CONV_W - 1):CONV_PAD, :] = c0_ref[...]

    x = x_ref[...]
    mod = mod_ref[...]
    hn = _rms_mod(x, ng_ref[...], mod[:, :, d:2 * d], mod[:, :, 0:d])
    z_scr[...] = _dot(hn.reshape(m, d), win_ref[...])

    xc, tail = _conv_taps(cbuf, z_scr[:, 0:w], cw_ref, cb_ref, nb, tt)
    c_ref[...] = tail
    half = w // 2
    r_parts, i_parts = [], []
    for j in range(2):
        gates = _dot(xc[:, j * half:(j + 1) * half], wg_ref[j])
        r_parts.append(_sigmoid(gates[:, :half] + gb_ref[0:1, j * half:(j + 1) * half]))
        i_parts.append(_sigmoid(gates[:, half:] + gb_ref[1:2, j * half:(j + 1) * half]))
    r = jnp.concatenate(r_parts, axis=1)
    i = jnp.concatenate(i_parts, axis=1)
    log_a = (-LRU_C * r) * _softplus(-lam_ref[...])
    a = jnp.exp(log_a)
    b = jnp.sqrt(-jnp.tanh(log_a) * (a * a + 1.0)) * (i * xc)
    pos = _row_pos((m, w), tt)
    h_prev = jnp.broadcast_to(h_ref[...], (nb, tt, w)).reshape(m, w)
    b = b + jnp.where(pos == 0, a * h_prev, 0.0)
    hs = _seg_linear_scan(a, b, pos, tt)
    for bi in range(nb):
        h_ref[bi] = hs[bi * tt + tt - 1:bi * tt + tt, :]
    mix_scr[:, 0:w] = (_gelu_tanh(z_scr[:, w:2 * w]) * hs).astype(BF16)

    cosv = cos_ref[...]
    sinv = sin_ref[...]
    for h in range(RET_HEADS):
        lo = h * RET_D
        qh = z_scr[:, q0 + lo:q0 + lo + RET_D]
        z_scr[:, q0 + lo:q0 + lo + RET_D] = qh * cosv + pltpu.roll(qh, RET_D // 2, 1) * sinv
        kh = z_scr[:, k0 + lo:k0 + lo + RET_D]
        z_scr[:, k0 + lo:k0 + lo + RET_D] = (kh * cosv + pltpu.roll(kh, RET_D // 2, 1) * sinv) * (RET_D ** -0.5)
    for bi in range(nb):
        for c in range(tt // chunk):
            r0 = bi * tt + c * chunk
            for h in range(RET_HEADS):
                lo = h * RET_D
                qh = z_scr[r0:r0 + chunk, q0 + lo:q0 + lo + RET_D]
                kh = z_scr[r0:r0 + chunk, k0 + lo:k0 + lo + RET_D]
                vh = z_scr[r0:r0 + chunk, v0 + lo:v0 + lo + RET_D]
                s_old = s_ref[bi, h]
                att = _dot_nt(qh, kh) * dec_ref[h]
                o = _dot(att, vh) + _dot(qh, s_old) * qdec_ref[h]
                s_ref[bi, h] = sdec[h] * s_old + _dot_tn(kh * kdec_ref[h], vh)
                z_scr[r0:r0 + chunk, q0 + lo:q0 + lo + RET_D] = o
    for h in range(RET_HEADS):
        lo = h * RET_D
        o = z_scr[:, q0 + lo:q0 + lo + RET_D]
        g = z_scr[:, g0 + lo:g0 + lo + RET_D]
        mix_scr[:, w + lo:w + lo + RET_D] = (_silu(g) * _head_norm(o, gn_ref[:, lo:lo + RET_D])).astype(BF16)

    y = jnp.dot(mix_scr[...], wout_ref[...], preferred_element_type=F32)
    y_ref[...] = x + mod[:, :, 2 * d:3 * d] * y.reshape(nb, tt, d)


def _retention_tables(chunk):
    h = np.arange(RET_HEADS, dtype=np.float64)
    log_g = np.log1p(-np.exp2(-5.0 - h))
    idx = np.arange(chunk, dtype=np.float64)
    diff = idx[:, None] - idx[None, :]
    decay = np.where(diff >= 0, np.exp(np.maximum(diff, 0.0)[None] * log_g[:, None, None]), 0.0)
    q_dec = np.exp((idx[None, :] + 1.0) * log_g[:, None])
    k_dec = np.exp((chunk - 1.0 - idx)[None, :] * log_g[:, None])
    s_dec = np.exp(chunk * log_g)
    lanes = np.ones((1, 1, RET_D))
    return (jnp.asarray(decay, F32), jnp.asarray(q_dec[:, :, None] * lanes, F32),
            jnp.asarray(k_dec[:, :, None] * lanes, F32), tuple(float(np.float32(s)) for s in s_dec))


def _rope_tables(pos, reps):
    half = RET_D // 2
    inv = ROPE_BASE ** (-np.arange(half, dtype=np.float64) / half)
    ang = pos.astype(np.float64)[:, None] * inv[None, :]
    cos = np.concatenate([np.cos(ang), np.cos(ang)], axis=-1)
    sin = np.concatenate([-np.sin(ang), np.sin(ang)], axis=-1)
    return jnp.asarray(np.tile(cos, (reps, 1)), F32), jnp.asarray(np.tile(sin, (reps, 1)), F32)


def _const_spec(shape):
    zeros = (0,) * len(shape)
    return pl.BlockSpec(shape, lambda b, t: zeros)


def _tiling(batch, seq):
    if seq >= 256:
        return 1, 256, min(CHUNK, seq)
    return batch, seq, min(CHUNK, seq)


def _even_mixer_call(x, mod, ng, w_in, w_out, conv_w, conv_b, wg, gb, lam, gn, pos0, h0, c0, s0):
    bsz, seq, d = x.shape
    nb, tt, chunk = _tiling(bsz, seq)
    m = nb * tt
    decay, q_dec, k_dec, s_dec = _retention_tables(chunk)
    cos, sin = _rope_tables(pos0 + np.arange(seq), nb)
    w = LRU_WIDTH
    state_spec = lambda shape: pl.BlockSpec((nb,) + shape, lambda b, t: (b,) + (0,) * len(shape))
    kern = functools.partial(_even_kernel, nb=nb, tt=tt, chunk=chunk, sdec=s_dec)
    return pl.pallas_call(
        kern,
        grid=(bsz // nb, seq // tt),
        in_specs=[
            pl.BlockSpec((nb, tt, d), lambda b, t: (b, t, 0)),
            pl.BlockSpec((nb, 1, 6 * d), lambda b, t: (b, 0, 0)),
            _const_spec((1, d)),
            _const_spec((d, EVEN_IN)),
            _const_spec((d, d)),
            _const_spec((CONV_W, w)),
            _const_spec((1, w)),
            _const_spec((2, w // 2, w)),
            _const_spec((2, w)),
            _const_spec((1, w)),
            _const_spec((1, RET_HEADS * RET_D)),
            pl.BlockSpec((m, RET_D), lambda b, t: (t, 0)),
            pl.BlockSpec((m, RET_D), lambda b, t: (t, 0)),
            _const_spec((RET_HEADS, chunk, chunk)),
            _const_spec((RET_HEADS, chunk, RET_D)),
            _const_spec((RET_HEADS, chunk, RET_D)),
            state_spec((1, w)),
            state_spec((CONV_W - 1, w)),
            state_spec((RET_HEADS, RET_D, RET_D)),
        ],
        out_specs=[
            pl.BlockSpec((nb, tt, d), lambda b, t: (b, t, 0)),
            state_spec((1, w)),
            state_spec((CONV_W - 1, w)),
            state_spec((RET_HEADS, RET_D, RET_D)),
        ],
        out_shape=[
            jax.ShapeDtypeStruct((bsz, seq, d), F32),
            jax.ShapeDtypeStruct((bsz, 1, w), F32),
            jax.ShapeDtypeStruct((bsz, CONV_W - 1, w), F32),
            jax.ShapeDtypeStruct((bsz, RET_HEADS, RET_D, RET_D), F32),
        ],
        scratch_shapes=[
            pltpu.VMEM((m, EVEN_IN), F32),
            pltpu.VMEM((nb, tt + CONV_PAD, w), F32),
            pltpu.VMEM((m, d), BF16),
        ],
        compiler_params=pltpu.CompilerParams(
            dimension_semantics=("arbitrary", "arbitrary"), vmem_limit_bytes=VMEM_LIMIT_BYTES),
        name="even_mixer",
    )(x, mod, ng, w_in, w_out, conv_w, conv_b, wg, gb, lam, gn, cos, sin, decay, q_dec, k_dec,
      h0, c0, s0)


_GQ, _GK, _GV, _GR, _U, _OP, _SM = 0, 256, 512, 1024, 1536, 2048, 2560
_SM_I, _SM_F = GLA_RANK, GLA_RANK + MLSTM_HEADS


def _odd_kernel(x_ref, mod_ref, ng_ref, win_ref, wout_ref, wgate_ref, bgate_ref, ggn_ref,
                cw_ref, cb_ref, wqk_ref, wv_ref, bi_ref, bf_ref, mgn_ref,
                s0_ref, c0_ref, n0_ref, m0_ref, u0_ref,
                y_ref, s_ref, c_ref, n_ref, m_ref, u_ref,
                z_scr, cbuf, qkv_scr, mix_scr, *, nb, tt, chunk):
    d = D_MODEL
    m = nb * tt
    wd = MLSTM_HEADS * MLSTM_DH
    dh = MLSTM_DH

    @pl.when(pl.program_id(1) == 0)
    def _():
        s_ref[...] = s0_ref[...]
        c_ref[...] = c0_ref[...]
        n_ref[...] = n0_ref[...]
        m_ref[...] = m0_ref[...]
        cbuf[:, CONV_PAD - (CONV_W - 1):CONV_PAD, :] = u0_ref[...]

    x = x_ref[...]
    mod = mod_ref[...]
    hn = _rms_mod(x, ng_ref[...], mod[:, :, d:2 * d], mod[:, :, 0:d])
    z_scr[...] = _dot(hn.reshape(m, d), win_ref[...])

    pos = _row_pos((m, 128), chunk)
    row = lax.broadcasted_iota(jnp.int32, (chunk, chunk), 0)
    col = lax.broadcasted_iota(jnp.int32, (chunk, chunk), 1)
    causal = row >= col
    diag = row == col
    lane = lax.broadcasted_iota(jnp.int32, (1, 128), 1)

    small = z_scr[:, _SM:_SM + 128]
    log_alpha = _log_sigmoid(_dot(small, wgate_ref[...]) + bgate_ref[...]) / GLA_TAU
    cum = jnp.concatenate(
        [_seg_cumsum(log_alpha[:, 0:128], pos, chunk), _seg_cumsum(log_alpha[:, 128:256], pos, chunk)], axis=1)
    z_scr[:, _GK:_GK + 256] = z_scr[:, _GK:_GK + 256] * (GLA_DK ** -0.5)
    for bi in range(nb):
        for c in range(tt // chunk):
            r0 = bi * tt + c * chunk
            cm = cum[r0:r0 + chunk, :]
            last = cm[chunk - 1:chunk, :]
            ref = 0.5 * last
            qc = z_scr[r0:r0 + chunk, _GQ:_GQ + 256]
            kc = z_scr[r0:r0 + chunk, _GK:_GK + 256]
            q_in = qc * jnp.exp(cm - ref)
            k_in = kc * jnp.exp(ref - cm)
            q_st = qc * jnp.exp(cm)
            k_st = kc * jnp.exp(last - cm)
            e_last = jnp.exp(last)
            for p in range(GLA_HEADS // 2):
                pl_lo = p * 128
                st_old = s_ref[bi, p]
                st_new = e_last[:, pl_lo:pl_lo + 128] * st_old
                for hh in range(2):
                    h = 2 * p + hh
                    own = (lane >= hh * GLA_DK) & (lane < (hh + 1) * GLA_DK)
                    vh = z_scr[r0:r0 + chunk, _GV + h * GLA_DV:_GV + (h + 1) * GLA_DV]
                    att = _dot_nt(jnp.where(own, q_in[:, pl_lo:pl_lo + 128], 0.0), k_in[:, pl_lo:pl_lo + 128])
                    att = jnp.where(causal, att, 0.0)
                    o = _dot(att, vh) + _dot_nt(jnp.where(own, q_st[:, pl_lo:pl_lo + 128], 0.0), st_old)
                    st_new = st_new + _dot_tn(vh, jnp.where(own, k_st[:, pl_lo:pl_lo + 128], 0.0))
                    z_scr[r0:r0 + chunk, _GV + h * GLA_DV:_GV + (h + 1) * GLA_DV] = o
                s_ref[bi, p] = st_new
    for h in range(GLA_HEADS):
        lo = h * GLA_DV
        o = z_scr[:, _GV + lo:_GV + lo + GLA_DV]
        g = z_scr[:, _GR + lo:_GR + lo + GLA_DV]
        mix_scr[:, lo:lo + GLA_DV] = (_silu(g) * _head_norm(o, ggn_ref[:, lo:lo + GLA_DV])).astype(BF16)

    u = z_scr[:, _U:_U + wd]
    uc, tail = _conv_taps(cbuf, u, cw_ref, cb_ref, nb, tt)
    u_ref[...] = tail
    uc = _silu(uc)
    for h in range(MLSTM_HEADS):
        lo = h * dh
        qk = _dot(uc[:, lo:lo + dh], wqk_ref[h])
        qkv_scr[:, lo:lo + dh] = qk[:, 0:dh]
        qkv_scr[:, wd + lo:wd + lo + dh] = qk[:, dh:2 * dh] * (dh ** -0.5)
        qkv_scr[:, 2 * wd + lo:2 * wd + lo + dh] = _dot(u[:, lo:lo + dh], wv_ref[h])
    ic_all = small + bi_ref[...]
    fc_all = _log_sigmoid(small + bf_ref[...])
    b_all = _seg_cumsum(fc_all, pos, chunk)
    for bi in range(nb):
        for c in range(tt // chunk):
            r0 = bi * tt + c * chunk
            m_row = m_ref[bi]
            m_new_row = m_row
            for h in range(MLSTM_HEADS):
                lo = h * dh
                qh = qkv_scr[r0:r0 + chunk, lo:lo + dh]
                kh = qkv_scr[r0:r0 + chunk, wd + lo:wd + lo + dh]
                vh = qkv_scr[r0:r0 + chunk, 2 * wd + lo:2 * wd + lo + dh]
                b_col = b_all[r0:r0 + chunk, _SM_F + h:_SM_F + h + 1]
                i_col = ic_all[r0:r0 + chunk, _SM_I + h:_SM_I + h + 1]
                m_prev = m_row[:, h:h + 1]
                a_col = b_col + m_prev
                g_col = i_col - b_col
                g_row = jnp.sum(jnp.where(diag, g_col, 0.0), axis=0, keepdims=True)
                dlog = jnp.where(causal, b_col + g_row, -jnp.inf)
                m_t = jnp.maximum(a_col, jnp.max(dlog, axis=-1, keepdims=True))
                wgt = jnp.exp(dlog - m_t)
                w_in = jnp.exp(a_col - m_t)
                qkw = _dot_nt(qh, kh) * wgt
                c_old = c_ref[bi, h]
                n_old = n_ref[bi, h:h + 1, :]
                num = _dot(qkw, vh) + w_in * _dot(qh, c_old)
                den = jnp.sum(qkw, axis=-1, keepdims=True) + w_in * jnp.sum(qh * n_old, axis=-1, keepdims=True)
                hc = num / jnp.maximum(jnp.abs(den), jnp.exp(-m_t))
                m_last = m_t[chunk - 1:chunk, :]
                w_in_last = w_in[chunk - 1:chunk, :]
                b_last = b_col[chunk - 1:chunk, :]
                w_last = jnp.exp((b_last - b_col) + i_col - m_last)
                kw = kh * w_last
                c_ref[bi, h] = w_in_last * c_old + _dot_tn(kw, vh)
                n_ref[bi, h:h + 1, :] = w_in_last * n_old + jnp.sum(kw, axis=0, keepdims=True)
                m_new_row = jnp.where(lane == h, m_last, m_new_row)
                qkv_scr[r0:r0 + chunk, lo:lo + dh] = hc
            m_ref[bi] = m_new_row
    for h in range(MLSTM_HEADS):
        lo = h * dh
        hm = qkv_scr[:, lo:lo + dh]
        og = z_scr[:, _OP + lo:_OP + lo + dh]
        mix_scr[:, wd + lo:wd + lo + dh] = (_sigmoid(og) * _head_norm(hm, mgn_ref[:, lo:lo + dh])).astype(BF16)

    y = jnp.dot(mix_scr[...], wout_ref[...], preferred_element_type=F32)
    y_ref[...] = x + mod[:, :, 2 * d:3 * d] * y.reshape(nb, tt, d)


def _odd_mixer_call(x, mod, ng, w_in, w_out, w_gate, b_gate, ggn, conv_w, conv_b, w_qk, w_v, b_i, b_f, mgn,
                    s0, c0, n0, m0, u0):
    bsz, seq, d = x.shape
    nb, tt, chunk = _tiling(bsz, seq)
    m = nb * tt
    wd = MLSTM_HEADS * MLSTM_DH
    dh = MLSTM_DH
    state_spec = lambda shape: pl.BlockSpec((nb,) + shape, lambda b, t: (b,) + (0,) * len(shape))
    state_shapes = [(GLA_HEADS // 2, GLA_DV, 2 * GLA_DK), (MLSTM_HEADS, dh, dh), (MLSTM_HEADS, dh), (1, 128),
                    (CONV_W - 1, wd)]
    kern = functools.partial(_odd_kernel, nb=nb, tt=tt, chunk=chunk)
    return pl.pallas_call(
        kern,
        grid=(bsz // nb, seq // tt),
        in_specs=[
            pl.BlockSpec((nb, tt, d), lambda b, t: (b, t, 0)),
            pl.BlockSpec((nb, 1, 6 * d), lambda b, t: (b, 0, 0)),
            _const_spec((1, d)),
            _const_spec((d, ODD_IN_PADDED)),
            _const_spec((d, d)),
            _const_spec((128, GLA_HEADS * GLA_DK)),
            _const_spec((1, GLA_HEADS * GLA_DK)),
            _const_spec((1, GLA_HEADS * GLA_DV)),
            _const_spec((CONV_W, wd)),
            _const_spec((1, wd)),
            _const_spec((MLSTM_HEADS, dh, 2 * dh)),
            _const_spec((MLSTM_HEADS, dh, dh)),
            _const_spec((1, 128)),
            _const_spec((1, 128)),
            _const_spec((1, wd)),
        ] + [state_spec(s) for s in state_shapes],
        out_specs=[pl.BlockSpec((nb, tt, d), lambda b, t: (b, t, 0))] + [state_spec(s) for s in state_shapes],
        out_shape=[jax.ShapeDtypeStruct((bsz, seq, d), F32)]
        + [jax.ShapeDtypeStruct((bsz,) + s, F32) for s in state_shapes],
        scratch_shapes=[
            pltpu.VMEM((m, ODD_IN_PADDED), F32),
            pltpu.VMEM((nb, tt + CONV_PAD, wd), F32),
            pltpu.VMEM((m, 3 * wd), F32),
            pltpu.VMEM((m, d), BF16),
        ],
        compiler_params=pltpu.CompilerParams(
            dimension_semantics=("arbitrary", "arbitrary"), vmem_limit_bytes=VMEM_LIMIT_BYTES),
        name="odd_mixer",
    )(x, mod, ng, w_in, w_out, w_gate, b_gate, ggn, conv_w, conv_b, w_qk, w_v, b_i, b_f, mgn,
      s0, c0, n0, m0, u0)


def _ffn_kernel(x_ref, mod_ref, ng_ref, w1_ref, w2_ref, fg_ref, o_ref, *, nb, tt, final):
    d = D_MODEL
    m = nb * tt
    x = x_ref[...]
    mod = mod_ref[...]
    hn = _rms_mod(x, ng_ref[...], mod[:, :, 4 * d:5 * d], mod[:, :, 3 * d:4 * d]).reshape(m, d).astype(BF16)
    acc = jnp.zeros((m, d), F32)
    for j in range(D_FF // FFN_COLS):
        lo = j * FFN_COLS
        gate = jnp.dot(hn, w1_ref[:, lo:lo + FFN_COLS], preferred_element_type=F32)
        up = jnp.dot(hn, w1_ref[:, D_FF + lo:D_FF + lo + FFN_COLS], preferred_element_type=F32)
        act = (_silu(gate) * up).astype(BF16)
        acc = acc + jnp.dot(act, w2_ref[lo:lo + FFN_COLS, :], preferred_element_type=F32)
    x2 = x + mod[:, :, 5 * d:6 * d] * acc.reshape(nb, tt, d)
    if final:
        x2 = (x2 * lax.rsqrt(jnp.mean(x2 * x2, axis=-1, keepdims=True) + NORM_EPS)) * fg_ref[...][None]
    o_ref[...] = x2


def _ffn_call(x, mod, ng, w1, w2, fg, final):
    bsz, seq, d = x.shape
    nb, tt = (1, 512) if seq >= 512 else (bsz, seq)
    kern = functools.partial(_ffn_kernel, nb=nb, tt=tt, final=final)
    return pl.pallas_call(
        kern,
        grid=(bsz // nb, seq // tt),
        in_specs=[
            pl.BlockSpec((nb, tt, d), lambda b, t: (b, t, 0)),
            pl.BlockSpec((nb, 1, 6 * d), lambda b, t: (b, 0, 0)),
            _const_spec((1, d)),
            _const_spec((d, 2 * D_FF)),
            _const_spec((D_FF, d)),
            _const_spec((1, d)),
        ],
        out_specs=pl.BlockSpec((nb, tt, d), lambda b, t: (b, t, 0)),
        out_shape=jax.ShapeDtypeStruct((bsz, seq, d), F32),
        compiler_params=pltpu.CompilerParams(
            dimension_semantics=("arbitrary", "arbitrary"), vmem_limit_bytes=VMEM_LIMIT_BYTES),
        name="swiglu_ffn",
    )(x, mod, ng, w1, w2, fg)


def _block_diag(blocks):
    n, r, c = blocks.shape
    eye = jnp.eye(n, dtype=blocks.dtype)
    return (eye[:, None, :, None] * blocks[:, :, None, :]).reshape(n * r, n * c)


def _lru_gate_layout(gate_w, gate_b):
    halves = []
    per = LRU_HEADS // 2
    for j in range(2):
        blk = gate_w[j * per:(j + 1) * per]
        halves.append(jnp.concatenate([_block_diag(blk[:, :, :LRU_BLOCK]), _block_diag(blk[:, :, LRU_BLOCK:])], axis=1))
    bias = jnp.stack([gate_b[:, :LRU_BLOCK].reshape(-1), gate_b[:, LRU_BLOCK:].reshape(-1)])
    return jnp.stack(halves).astype(BF16), bias


def _odd_in_layout(w):
    pad = jnp.zeros((w.shape[0], 128 - GLA_RANK - 2 * MLSTM_HEADS), w.dtype)
    return jnp.concatenate([w[:, 0:1024], w[:, 1040:2576], w[:, 1024:1040], w[:, 2576:2584], pad], axis=1).astype(BF16)


def _lane_row(vec, offset):
    return jnp.zeros((1, 128), F32).at[0, offset:offset + vec.shape[0]].set(vec)


def _gla_state_in(s):
    b = s.shape[0]
    return jnp.swapaxes(s.reshape(b, GLA_HEADS // 2, 2 * GLA_DK, GLA_DV), -1, -2)


def _gla_state_out(s):
    b = s.shape[0]
    return jnp.swapaxes(s, -1, -2).reshape(b, GLA_HEADS, GLA_DK, GLA_DV)


def _trunk(x, mods, states, P, pos0):
    bsz = x.shape[0]
    lru_h, lru_conv, ret, gla, ml_c, ml_n, ml_m, ml_conv = states
    x, h_new, conv_new, ret_new = _even_mixer_call(
        x, mods[0], P['norm_mix_g'][0:1], P['a_w_in'], P['a_w_out'], P['lru_conv_w'], P['lru_conv_b'],
        P['lru_wg'], P['lru_gb'], P['lru_lambda'], P['ret_gn_g'], pos0,
        lru_h.reshape(bsz, 1, LRU_WIDTH), lru_conv, ret)
    x = _ffn_call(x, mods[0], P['norm_ffn_g'][0:1], P['w_ffn_in'][0], P['w_ffn_out'][0], P['final_norm_g'], False)
    m_in = jnp.pad(ml_m, ((0, 0), (0, 128 - MLSTM_HEADS))).reshape(bsz, 1, 128)
    x, gla_new, c_new, n_new, m_new, mconv_new = _odd_mixer_call(
        x, mods[1], P['norm_mix_g'][1:2], P['c_w_in'], P['c_w_out'], P['gla_w_gate'], P['gla_b_gate'],
        P['gla_gn_g'], P['mlstm_conv_w'], P['mlstm_conv_b'], P['mlstm_w_qk'], P['mlstm_w_v'],
        P['mlstm_b_i'], P['mlstm_b_f'], P['mlstm_gn_g'],
        _gla_state_in(gla), ml_c, ml_n, m_in, ml_conv)
    y = _ffn_call(x, mods[1], P['norm_ffn_g'][1:2], P['w_ffn_in'][1], P['w_ffn_out'][1], P['final_norm_g'], True)
    new_states = (h_new.reshape(1, bsz, LRU_WIDTH), conv_new[None], ret_new[None], _gla_state_out(gla_new)[None],
                  c_new[None], n_new[None], m_new[:, 0, :MLSTM_HEADS][None], mconv_new[None])
    return y, new_states


def kernel(x_prompt, x_sample, c_prompt, c_sample, state_lru_h, state_lru_conv, state_ret, state_gla, state_mlstm_C, state_mlstm_n, state_mlstm_m, state_mlstm_conv, w_mod, b_mod, norm_mix_g, norm_ffn_g, w_ffn_in, w_ffn_out, final_norm_g, a_w_in, a_w_out, lru_conv_w, lru_conv_b, lru_gate_w, lru_gate_b, lru_lambda, ret_gn_g, c_w_in, c_w_out, gla_w_gate, gla_b_gate, gla_gn_g, mlstm_conv_w, mlstm_conv_b, mlstm_w_qk, mlstm_w_v, mlstm_b_if, mlstm_gn_g):
    bp = x_prompt.shape[0]
    bs = x_sample.shape[0]
    lru_wg, lru_gb = _lru_gate_layout(lru_gate_w[0], lru_gate_b[0])
    w_gate_pad = jnp.zeros((128, GLA_HEADS * GLA_DK), F32).at[0:GLA_RANK].set(gla_w_gate[0]).astype(BF16)
    P = {
        'norm_mix_g': norm_mix_g, 'norm_ffn_g': norm_ffn_g, 'final_norm_g': final_norm_g.reshape(1, D_MODEL),
        'w_ffn_in': w_ffn_in.astype(BF16), 'w_ffn_out': w_ffn_out.astype(BF16),
        'a_w_in': a_w_in[0].astype(BF16), 'a_w_out': a_w_out[0].astype(BF16),
        'lru_conv_w': lru_conv_w[0], 'lru_conv_b': lru_conv_b, 'lru_wg': lru_wg, 'lru_gb': lru_gb,
        'lru_lambda': lru_lambda, 'ret_gn_g': ret_gn_g,
        'c_w_in': _odd_in_layout(c_w_in[0]), 'c_w_out': c_w_out[0].astype(BF16),
        'gla_w_gate': w_gate_pad, 'gla_b_gate': gla_b_gate, 'gla_gn_g': gla_gn_g,
        'mlstm_conv_w': mlstm_conv_w[0], 'mlstm_conv_b': mlstm_conv_b,
        'mlstm_w_qk': mlstm_w_qk[0].astype(BF16), 'mlstm_w_v': mlstm_w_v[0].astype(BF16),
        'mlstm_b_i': _lane_row(mlstm_b_if[0, :MLSTM_HEADS], _SM_I),
        'mlstm_b_f': _lane_row(mlstm_b_if[0, MLSTM_HEADS:], _SM_F),
        'mlstm_gn_g': mlstm_gn_g,
    }
    mod = _modulation(jnp.concatenate([c_prompt, c_sample], axis=0), w_mod, b_mod)
    mods_p = [mod[l, :bp].reshape(bp, 1, 6 * D_MODEL) for l in range(2)]
    mods_s = [mod[l, bp:].reshape(bs, 1, 6 * D_MODEL) for l in range(2)]

    zeros = lambda *shape: jnp.zeros(shape, F32)
    wd = MLSTM_HEADS * MLSTM_DH
    states_p = (zeros(bp, LRU_WIDTH), zeros(bp, CONV_W - 1, LRU_WIDTH), zeros(bp, RET_HEADS, RET_D, RET_D),
                zeros(bp, GLA_HEADS, GLA_DK, GLA_DV), zeros(bp, MLSTM_HEADS, MLSTM_DH, MLSTM_DH),
                zeros(bp, MLSTM_HEADS, MLSTM_DH), zeros(bp, MLSTM_HEADS), zeros(bp, CONV_W - 1, wd))
    states_s = (state_lru_h[0], state_lru_conv[0], state_ret[0], state_gla[0], state_mlstm_C[0],
                state_mlstm_n[0], state_mlstm_m[0], state_mlstm_conv[0])
    y_p, sp = _trunk(x_prompt, mods_p, states_p, P, 0)
    y_s, ss = _trunk(x_sample, mods_s, states_s, P, PAST_LEN)
    return (y_p, y_s) + sp + ss
```

```python
import functools

import numpy as np
import jax
import jax.numpy as jnp
from jax import lax
from jax.experimental import pallas as pl
from jax.experimental.pallas import tpu as pltpu

F32 = jnp.float32
BF16 = jnp.bfloat16

D_MODEL = 1024
PAST_LEN = 1024
CHUNK = 64
CONV_W = 4
NORM_EPS = 1e-6
LRU_WIDTH = 512
LRU_HEADS = 8
LRU_BLOCK = 64
LRU_C = 8.0
RET_HEADS = 4
RET_D = 128
ROPE_BASE = 10000.0
GLA_HEADS = 4
GLA_DK = 64
GLA_DV = 128
GLA_RANK = 16
GLA_TAU = 16.0
MLSTM_HEADS = 4
MLSTM_DH = 128
D_FF = 2816
EVEN_IN = 3072
ODD_IN_PADDED = 2688
FFN_COLS = 256

VMEM_LIMIT_BYTES = 56 * 1024 * 1024
CONV_PAD = 8


def _dot(a, b):
    return jnp.dot(a.astype(BF16), b.astype(BF16), preferred_element_type=F32)


def _dot_nt(a, b):
    return lax.dot_general(a.astype(BF16), b.astype(BF16), (((1,), (1,)), ((), ())),
                           preferred_element_type=F32)


def _dot_tn(a, b):
    return lax.dot_general(a.astype(BF16), b.astype(BF16), (((0,), (0,)), ((), ())),
                           preferred_element_type=F32)


def _sigmoid(x):
    return jax.nn.sigmoid(x)


def _silu(x):
    return x * jax.nn.sigmoid(x)


def _softplus(x):
    return jnp.maximum(x, 0.0) + jnp.log1p(jnp.exp(-jnp.abs(x)))


def _log_sigmoid(x):
    return -_softplus(-x)


def _gelu_tanh(x):
    c = np.sqrt(2.0 / np.pi).astype(np.float32)
    return 0.5 * x * (1.0 + jnp.tanh(c * (x + 0.044715 * (x * x * x))))


def _rms_mod(x, gain, scale, shift):
    y = x * lax.rsqrt(jnp.mean(x * x, axis=-1, keepdims=True) + NORM_EPS)
    return (y * gain[None]) * (1.0 + scale) + shift


def _head_norm(o, gain):
    oc = o - jnp.mean(o, axis=-1, keepdims=True)
    y = oc * lax.rsqrt(jnp.mean(oc * oc, axis=-1, keepdims=True) + NORM_EPS)
    return y * gain


def _row_pos(shape, period):
    return jnp.bitwise_and(lax.broadcasted_iota(jnp.int32, shape, 0), period - 1)


def _seg_cumsum(x, pos, length):
    d = 1
    while d < length:
        x = x + jnp.where(pos >= d, pltpu.roll(x, d, 0), 0.0)
        d *= 2
    return x


def _linear_scan(a, b, h0_rows, seg):
    m = a.shape[0]
    pos = _row_pos(a.shape, 8)
    for d in (1, 2, 4):
        valid = pos >= d
        b = b + jnp.where(valid, a * pltpu.roll(b, d, 0), 0.0)
        a = a * jnp.where(valid, pltpu.roll(a, d, 0), 1.0)
    out = []
    for g in range(m // 8):
        if (g * 8) % seg == 0:
            h = h0_rows[(g * 8) // seg]
        rows = b[g * 8:g * 8 + 8] + a[g * 8:g * 8 + 8] * h
        out.append(rows)
        h = rows[7:8]
    return jnp.concatenate(out, axis=0)


def _conv_taps(cbuf, xin, w_ref, b_ref, nb, tt):
    width = xin.shape[-1]
    cbuf[:, CONV_PAD:CONV_PAD + tt, :] = xin.reshape(nb, tt, width)
    base = CONV_PAD - (CONV_W - 1)
    out = b_ref[...][None] + w_ref[0:1, :][None] * cbuf[:, base:base + tt, :]
    for tap in range(1, CONV_W):
        out = out + w_ref[tap:tap + 1, :][None] * cbuf[:, base + tap:base + tap + tt, :]
    tail = cbuf[:, base + tt:CONV_PAD + tt, :]
    cbuf[:, base:CONV_PAD, :] = tail
    return out.reshape(nb * tt, width), tail


def _mod_kernel(c_ref, w_ref, b_ref, o_ref):
    c = c_ref[...]
    o_ref[0] = _dot(_silu(c), w_ref[0]) + b_ref[0]


def _modulation(c_all, w_mod, b_mod):
    depth, d, n = w_mod.shape
    rows = c_all.shape[0]
    tn = 1024
    return pl.pallas_call(
        _mod_kernel,
        grid=(depth, n // tn),
        in_specs=[
            pl.BlockSpec((rows, d), lambda l, j: (0, 0)),
            pl.BlockSpec((1, d, tn), lambda l, j: (l, 0, j)),
            pl.BlockSpec((1, 1, tn), lambda l, j: (l, 0, j)),
        ],
        out_specs=pl.BlockSpec((1, rows, tn), lambda l, j: (l, 0, j)),
        out_shape=jax.ShapeDtypeStruct((depth, rows, n), F32),
        compiler_params=pltpu.CompilerParams(
            dimension_semantics=("arbitrary", "arbitrary"), vmem_limit_bytes=VMEM_LIMIT_BYTES),
        name="adaln_modulation",
    )(c_all, w_mod, b_mod.reshape(depth, 1, n))


def _even_kernel(x_ref, mod_ref, ng_ref, win_ref, wout_ref, cw_ref, cb_ref, wg_ref, gb_ref, lam_ref,
                 gn_ref, cos_ref, sin_ref, dec_ref, qdec_ref, kdec_ref, h0_ref, c0_ref, s0_ref,
                 y_ref, h_ref, c_ref, s_ref, cbuf, *, nb, tt, sdec):
    d = D_MODEL
    m = nb * tt
    w = LRU_WIDTH
    q0, k0, v0, g0 = 2 * w, 3 * w, 4 * w, 5 * w

    @pl.when(pl.program_id(1) == 0)
    def _():
        h_ref[...] = h0_ref[...]
        s_ref[...] = s0_ref[...]
        cbuf[:, CONV_PAD - (CONV_W - 1):CONV_PAD, :] = c0_ref[...]

    x = x_ref[...]
    mod = mod_ref[...]
    hn = _rms_mod(x, ng_ref[...], mod[:, :, d:2 * d], mod[:, :, 0:d]).reshape(m, d).astype(BF16)
    proj = lambda lo, width: jnp.dot(hn, win_ref[:, lo:lo + width], preferred_element_type=F32)

    xc, tail = _conv_taps(cbuf, proj(0, w), cw_ref, cb_ref, nb, tt)
    c_ref[...] = tail
    q, k = proj(q0, w), proj(k0, w)
    half = w // 2
    r_parts, i_parts = [], []
    for j in range(2):
        gates = _dot(xc[:, j * half:(j + 1) * half], wg_ref[j])
        r_parts.append(_sigmoid(gates[:, :half] + gb_ref[0:1, j * half:(j + 1) * half]))
        i_parts.append(_sigmoid(gates[:, half:] + gb_ref[1:2, j * half:(j + 1) * half]))
    r = jnp.concatenate(r_parts, axis=1)
    i = jnp.concatenate(i_parts, axis=1)
    log_a = (-LRU_C * r) * _softplus(-lam_ref[...])
    a = jnp.exp(log_a)
    b = jnp.sqrt(-jnp.tanh(log_a) * (a * a + 1.0)) * (i * xc)
    v = proj(v0, w)

    cosv = cos_ref[...]
    sinv = sin_ref[...]
    qr, kr = [], []
    for h in range(RET_HEADS):
        lo = h * RET_D
        qh = q[:, lo:lo + RET_D]
        kh = k[:, lo:lo + RET_D]
        qr.append(qh * cosv + pltpu.roll(qh, RET_D // 2, 1) * sinv)
        kr.append((kh * cosv + pltpu.roll(kh, RET_D // 2, 1) * sinv) * (RET_D ** -0.5))

    hs = _linear_scan(a, b, [h_ref[bi] for bi in range(nb)], tt)
    for bi in range(nb):
        h_ref[bi] = hs[bi * tt + tt - 1:bi * tt + tt, :]

    units = [(h, bi) for h in range(RET_HEADS) for bi in range(nb)]
    att, s_old = {}, {}
    for (h, bi) in units:
        rows = slice(bi * tt, (bi + 1) * tt)
        att[h, bi] = _dot_nt(qr[h][rows], kr[h][rows]) * dec_ref[h]
        s_old[h, bi] = s_ref[bi, h]
    o_parts = {}
    for (h, bi) in units:
        rows = slice(bi * tt, (bi + 1) * tt)
        vh = v[rows, h * RET_D:(h + 1) * RET_D]
        o_parts[h, bi] = _dot(att[h, bi], vh) + _dot(qr[h][rows], s_old[h, bi]) * qdec_ref[h]
        s_ref[bi, h] = sdec[h] * s_old[h, bi] + _dot_tn(kr[h][rows] * kdec_ref[h], vh)
    o_heads = [o_parts[h, 0] if nb == 1 else jnp.concatenate([o_parts[h, bi] for bi in range(nb)], axis=0)
               for h in range(RET_HEADS)]

    gb, g = proj(w, w), proj(g0, w)
    mix = [(_gelu_tanh(gb) * hs).astype(BF16)]
    for h in range(RET_HEADS):
        lo = h * RET_D
        mix.append((_silu(g[:, lo:lo + RET_D]) * _head_norm(o_heads[h], gn_ref[:, lo:lo + RET_D])).astype(BF16))

    y = jnp.dot(jnp.concatenate(mix, axis=1), wout_ref[...], preferred_element_type=F32)
    y_ref[...] = x + mod[:, :, 2 * d:3 * d] * y.reshape(nb, tt, d)


def _retention_tables(chunk):
    h = np.arange(RET_HEADS, dtype=np.float64)
    log_g = np.log1p(-np.exp2(-5.0 - h))
    idx = np.arange(chunk, dtype=np.float64)
    diff = idx[:, None] - idx[None, :]
    decay = np.where(diff >= 0, np.exp(np.maximum(diff, 0.0)[None] * log_g[:, None, None]), 0.0)
    q_dec = np.exp((idx[None, :] + 1.0) * log_g[:, None])
    k_dec = np.exp((chunk - 1.0 - idx)[None, :] * log_g[:, None])
    s_dec = np.exp(chunk * log_g)
    lanes = np.ones((1, 1, RET_D))
    return (jnp.asarray(decay, F32), jnp.asarray(q_dec[:, :, None] * lanes, F32),
            jnp.asarray(k_dec[:, :, None] * lanes, F32), tuple(float(np.float32(s)) for s in s_dec))


def _rope_tables(pos, reps):
    half = RET_D // 2
    inv = ROPE_BASE ** (-np.arange(half, dtype=np.float64) / half)
    ang = pos.astype(np.float64)[:, None] * inv[None, :]
    cos = np.concatenate([np.cos(ang), np.cos(ang)], axis=-1)
    sin = np.concatenate([-np.sin(ang), np.sin(ang)], axis=-1)
    return jnp.asarray(np.tile(cos, (reps, 1)), F32), jnp.asarray(np.tile(sin, (reps, 1)), F32)


def _const_spec(shape):
    zeros = (0,) * len(shape)
    return pl.BlockSpec(shape, lambda b, t: zeros)


def _tiling(batch, seq):
    if seq >= 256:
        return 1, 256, min(CHUNK, seq)
    return batch, seq, min(CHUNK, seq)


def _even_mixer_call(x, mod, ng, w_in, w_out, conv_w, conv_b, wg, gb, lam, gn, pos0, h0, c0, s0):
    bsz, seq, d = x.shape
    nb, tt, _ = _tiling(bsz, seq)
    chunk = tt
    m = nb * tt
    decay, q_dec, k_dec, s_dec = _retention_tables(chunk)
    cos, sin = _rope_tables(pos0 + np.arange(seq), nb)
    w = LRU_WIDTH
    state_spec = lambda shape: pl.BlockSpec((nb,) + shape, lambda b, t: (b,) + (0,) * len(shape))
    kern = functools.partial(_even_kernel, nb=nb, tt=tt, sdec=s_dec)
    return pl.pallas_call(
        kern,
        grid=(bsz // nb, seq // tt),
        in_specs=[
            pl.BlockSpec((nb, tt, d), lambda b, t: (b, t, 0)),
            pl.BlockSpec((nb, 1, 6 * d), lambda b, t: (b, 0, 0)),
            _const_spec((1, d)),
            _const_spec((d, EVEN_IN)),
            _const_spec((d, d)),
            _const_spec((CONV_W, w)),
            _const_spec((1, w)),
            _const_spec((2, w // 2, w)),
            _const_spec((2, w)),
            _const_spec((1, w)),
            _const_spec((1, RET_HEADS * RET_D)),
            pl.BlockSpec((m, RET_D), lambda b, t: (t, 0)),
            pl.BlockSpec((m, RET_D), lambda b, t: (t, 0)),
            _const_spec((RET_HEADS, chunk, chunk)),
            _const_spec((RET_HEADS, chunk, RET_D)),
            _const_spec((RET_HEADS, chunk, RET_D)),
            state_spec((1, w)),
            state_spec((CONV_W - 1, w)),
            state_spec((RET_HEADS, RET_D, RET_D)),
        ],
        out_specs=[
            pl.BlockSpec((nb, tt, d), lambda b, t: (b, t, 0)),
            state_spec((1, w)),
            state_spec((CONV_W - 1, w)),
            state_spec((RET_HEADS, RET_D, RET_D)),
        ],
        out_shape=[
            jax.ShapeDtypeStruct((bsz, seq, d), F32),
            jax.ShapeDtypeStruct((bsz, 1, w), F32),
            jax.ShapeDtypeStruct((bsz, CONV_W - 1, w), F32),
            jax.ShapeDtypeStruct((bsz, RET_HEADS, RET_D, RET_D), F32),
        ],
        scratch_shapes=[pltpu.VMEM((nb, tt + CONV_PAD, w), F32)],
        compiler_params=pltpu.CompilerParams(
            dimension_semantics=("arbitrary", "arbitrary"), vmem_limit_bytes=VMEM_LIMIT_BYTES),
        name="even_mixer",
    )(x, mod, ng, w_in, w_out, conv_w, conv_b, wg, gb, lam, gn, cos, sin, decay, q_dec, k_dec,
      h0, c0, s0)


_GQ, _GK, _GV, _GR, _U, _OP, _SM = 0, 256, 512, 1024, 1536, 2048, 2560
_SM_I, _SM_F = GLA_RANK, GLA_RANK + MLSTM_HEADS


def _odd_kernel(x_ref, mod_ref, ng_ref, win_ref, wout_ref, wgate_ref, bgate_ref, ggn_ref,
                cw_ref, cb_ref, wqk_ref, wv_ref, bi_ref, bf_ref, mgn_ref,
                s0_ref, c0_ref, n0_ref, m0_ref, u0_ref,
                y_ref, s_ref, c_ref, n_ref, m_ref, u_ref,
                cbuf, *, nb, tt, gchunk, mchunk):
    d = D_MODEL
    m = nb * tt
    wd = MLSTM_HEADS * MLSTM_DH
    dh = MLSTM_DH

    @pl.when(pl.program_id(1) == 0)
    def _():
        s_ref[...] = s0_ref[...]
        c_ref[...] = c0_ref[...]
        n_ref[...] = n0_ref[...]
        m_ref[...] = m0_ref[...]
        cbuf[:, CONV_PAD - (CONV_W - 1):CONV_PAD, :] = u0_ref[...]

    x = x_ref[...]
    mod = mod_ref[...]
    hn = _rms_mod(x, ng_ref[...], mod[:, :, d:2 * d], mod[:, :, 0:d]).reshape(m, d).astype(BF16)

    proj = lambda lo, width: jnp.dot(hn, win_ref[:, lo:lo + width], preferred_element_type=F32)

    lane = lax.broadcasted_iota(jnp.int32, (1, 128), 1)

    def tri(n):
        row = lax.broadcasted_iota(jnp.int32, (n, n), 0)
        col = lax.broadcasted_iota(jnp.int32, (n, n), 1)
        return row >= col, row == col

    small = proj(_SM, 128)
    u = proj(_U, wd)
    uc, tail = _conv_taps(cbuf, u, cw_ref, cb_ref, nb, tt)
    u_ref[...] = tail
    uc = _silu(uc)
    gq = proj(_GQ, 256)
    gk = proj(_GK, 256) * (GLA_DK ** -0.5)
    log_alpha = _log_sigmoid(_dot(small, wgate_ref[...]) + bgate_ref[...]) / GLA_TAU
    gpos = _row_pos((m, 128), gchunk)
    cum = jnp.concatenate(
        [_seg_cumsum(log_alpha[:, 0:128], gpos, gchunk), _seg_cumsum(log_alpha[:, 128:256], gpos, gchunk)], axis=1)
    mq, mk, mv = [], [], []
    for h in range(MLSTM_HEADS):
        lo = h * dh
        qk = _dot(uc[:, lo:lo + dh], wqk_ref[h])
        mq.append(qk[:, 0:dh])
        mk.append(qk[:, dh:2 * dh] * (dh ** -0.5))
        mv.append(_dot(u[:, lo:lo + dh], wv_ref[h]))
    gv = proj(_GV, GLA_HEADS * GLA_DV)
    ic_all = small + bi_ref[...]
    fc_all = _log_sigmoid(small + bf_ref[...])
    b_all = _seg_cumsum(fc_all, _row_pos((m, 128), mchunk), mchunk)

    n_gc = tt // gchunk
    n_mc = tt // mchunk
    g_units = [(bi, c) for bi in range(nb) for c in range(n_gc)]
    m_units = [(bi, c) for bi in range(nb) for c in range(n_mc)]
    g_causal, _ = tri(gchunk)
    m_causal, m_diag = tri(mchunk)
    own = [(lane >= hh * GLA_DK) & (lane < (hh + 1) * GLA_DK) for hh in range(2)]

    mA = {}
    for (bi, c) in m_units:
        rows = slice(bi * tt + c * mchunk, bi * tt + (c + 1) * mchunk)
        for h in range(MLSTM_HEADS):
            b_col = b_all[rows, _SM_F + h:_SM_F + h + 1]
            i_col = ic_all[rows, _SM_I + h:_SM_I + h + 1]
            g_col = i_col - b_col
            g_row = jnp.sum(jnp.where(m_diag, g_col, 0.0), axis=0, keepdims=True)
            dlog = jnp.where(m_causal, b_col + g_row, -jnp.inf)
            mA[bi, c, h] = (b_col, i_col, dlog, jnp.max(dlog, axis=-1, keepdims=True),
                            _dot_nt(mq[h][rows], mk[h][rows]))

    gA = {}
    for (bi, c) in g_units:
        rows = slice(bi * tt + c * gchunk, bi * tt + (c + 1) * gchunk)
        cm = cum[rows]
        last = cm[gchunk - 1:gchunk, :]
        ref = 0.5 * last
        qc, kc = gq[rows], gk[rows]
        gA[bi, c] = (qc * jnp.exp(cm - ref), kc * jnp.exp(ref - cm), qc * jnp.exp(cm), kc * jnp.exp(last - cm),
                     jnp.exp(last))

    gB = {}
    for (bi, c) in g_units:
        rows = slice(bi * tt + c * gchunk, bi * tt + (c + 1) * gchunk)
        q_in, k_in, q_st, k_st, e_last = gA[bi, c]
        for h in range(GLA_HEADS):
            p, hh = divmod(h, 2)
            lanes = slice(p * 128, (p + 1) * 128)
            vh = gv[rows, h * GLA_DV:(h + 1) * GLA_DV]
            att = jnp.where(g_causal, _dot_nt(jnp.where(own[hh], q_in[:, lanes], 0.0), k_in[:, lanes]), 0.0)
            gB[bi, c, h] = (att, _dot_tn(vh, jnp.where(own[hh], k_st[:, lanes], 0.0)), vh)

    mB = {}
    for bi in range(nb):
        m_row = m_ref[bi]
        for c in range(n_mc):
            m_new_row = m_row
            for h in range(MLSTM_HEADS):
                b_col, i_col, dlog, rmax, qk = mA[bi, c, h]
                a_col = b_col + m_row[:, h:h + 1]
                m_t = jnp.maximum(a_col, rmax)
                m_last = m_t[mchunk - 1:mchunk, :]
                w_in = jnp.exp(a_col - m_t)
                w_last = jnp.exp((b_col[mchunk - 1:mchunk, :] - b_col) + i_col - m_last)
                mB[bi, c, h] = (qk * jnp.exp(dlog - m_t), w_in, w_last, jnp.exp(-m_t))
                m_new_row = jnp.where(lane == h, m_last, m_new_row)
            m_row = m_new_row
        m_ref[bi] = m_row

    gC = {}
    for (bi, c) in g_units:
        for h in range(GLA_HEADS):
            att, inc, vh = gB[bi, c, h]
            gC[bi, c, h] = _dot(att, vh)
    mC = {}
    for (bi, c) in m_units:
        rows = slice(bi * tt + c * mchunk, bi * tt + (c + 1) * mchunk)
        for h in range(MLSTM_HEADS):
            qkw, w_in, w_last, e_neg = mB[bi, c, h]
            kw = mk[h][rows] * w_last
            mC[bi, c, h] = (_dot(qkw, mv[h][rows]), jnp.sum(qkw, axis=-1, keepdims=True),
                            _dot_tn(kw, mv[h][rows]), jnp.sum(kw, axis=0, keepdims=True))

    gla_o = [[] for _ in range(GLA_HEADS)]
    ml_o = [[] for _ in range(MLSTM_HEADS)]
    for bi in range(nb):
        states = [s_ref[bi, p] for p in range(GLA_HEADS // 2)]
        for c in range(n_gc):
            q_st, e_last = gA[bi, c][2], gA[bi, c][4]
            for h in range(GLA_HEADS):
                p, hh = divmod(h, 2)
                lanes = slice(p * 128, (p + 1) * 128)
                gla_o[h].append(gC[bi, c, h] + _dot_nt(jnp.where(own[hh], q_st[:, lanes], 0.0), states[p]))
            for p in range(GLA_HEADS // 2):
                states[p] = (e_last[:, p * 128:(p + 1) * 128] * states[p] + gB[bi, c, 2 * p][1]) + gB[bi, c, 2 * p + 1][1]
        for p in range(GLA_HEADS // 2):
            s_ref[bi, p] = states[p]
        for h in range(MLSTM_HEADS):
            c_st = c_ref[bi, h]
            n_st = n_ref[bi, h:h + 1, :]
            for c in range(n_mc):
                rows = slice(bi * tt + c * mchunk, bi * tt + (c + 1) * mchunk)
                qkw, w_in, w_last, e_neg = mB[bi, c, h]
                num1, rsum, c_inc, n_inc = mC[bi, c, h]
                qh = mq[h][rows]
                num = num1 + w_in * _dot(qh, c_st)
                den = rsum + w_in * jnp.sum(qh * n_st, axis=-1, keepdims=True)
                ml_o[h].append(num / jnp.maximum(jnp.abs(den), e_neg))
                w_in_last = w_in[mchunk - 1:mchunk, :]
                c_st = w_in_last * c_st + c_inc
                n_st = w_in_last * n_st + n_inc
            c_ref[bi, h] = c_st
            n_ref[bi, h:h + 1, :] = n_st

    cat = lambda parts: parts[0] if len(parts) == 1 else jnp.concatenate(parts, axis=0)
    gr = proj(_GR, GLA_HEADS * GLA_DV)
    og = proj(_OP, wd)
    mix = []
    for h in range(GLA_HEADS):
        lo = h * GLA_DV
        mix.append((_silu(gr[:, lo:lo + GLA_DV]) * _head_norm(cat(gla_o[h]), ggn_ref[:, lo:lo + GLA_DV])).astype(BF16))
    for h in range(MLSTM_HEADS):
        lo = h * dh
        mix.append((_sigmoid(og[:, lo:lo + dh]) * _head_norm(cat(ml_o[h]), mgn_ref[:, lo:lo + dh])).astype(BF16))

    y = jnp.dot(jnp.concatenate(mix, axis=1), wout_ref[...], preferred_element_type=F32)
    y_ref[...] = x + mod[:, :, 2 * d:3 * d] * y.reshape(nb, tt, d)


def _odd_mixer_call(x, mod, ng, w_in, w_out, w_gate, b_gate, ggn, conv_w, conv_b, w_qk, w_v, b_i, b_f, mgn,
                    s0, c0, n0, m0, u0):
    bsz, seq, d = x.shape
    nb, tt, chunk = _tiling(bsz, seq)
    m = nb * tt
    wd = MLSTM_HEADS * MLSTM_DH
    dh = MLSTM_DH
    state_spec = lambda shape: pl.BlockSpec((nb,) + shape, lambda b, t: (b,) + (0,) * len(shape))
    state_shapes = [(GLA_HEADS // 2, GLA_DV, 2 * GLA_DK), (MLSTM_HEADS, dh, dh), (MLSTM_HEADS, dh), (1, 128),
                    (CONV_W - 1, wd)]
    kern = functools.partial(_odd_kernel, nb=nb, tt=tt, gchunk=chunk, mchunk=min(2 * chunk, tt))
    return pl.pallas_call(
        kern,
        grid=(bsz // nb, seq // tt),
        in_specs=[
            pl.BlockSpec((nb, tt, d), lambda b, t: (b, t, 0)),
            pl.BlockSpec((nb, 1, 6 * d), lambda b, t: (b, 0, 0)),
            _const_spec((1, d)),
            _const_spec((d, ODD_IN_PADDED)),
            _const_spec((d, d)),
            _const_spec((128, GLA_HEADS * GLA_DK)),
            _const_spec((1, GLA_HEADS * GLA_DK)),
            _const_spec((1, GLA_HEADS * GLA_DV)),
            _const_spec((CONV_W, wd)),
            _const_spec((1, wd)),
            _const_spec((MLSTM_HEADS, dh, 2 * dh)),
            _const_spec((MLSTM_HEADS, dh, dh)),
            _const_spec((1, 128)),
            _const_spec((1, 128)),
            _const_spec((1, wd)),
        ] + [state_spec(s) for s in state_shapes],
        out_specs=[pl.BlockSpec((nb, tt, d), lambda b, t: (b, t, 0))] + [state_spec(s) for s in state_shapes],
        out_shape=[jax.ShapeDtypeStruct((bsz, seq, d), F32)]
        + [jax.ShapeDtypeStruct((bsz,) + s, F32) for s in state_shapes],
        scratch_shapes=[pltpu.VMEM((nb, tt + CONV_PAD, wd), F32)],
        compiler_params=pltpu.CompilerParams(
            dimension_semantics=("arbitrary", "arbitrary"), vmem_limit_bytes=VMEM_LIMIT_BYTES),
        name="odd_mixer",
    )(x, mod, ng, w_in, w_out, w_gate, b_gate, ggn, conv_w, conv_b, w_qk, w_v, b_i, b_f, mgn,
      s0, c0, n0, m0, u0)


def _ffn_kernel(x_ref, mod_ref, ng_ref, w1_ref, w2_ref, fg_ref, o_ref, *, nb, tt, final):
    d = D_MODEL
    m = nb * tt
    x = x_ref[...]
    mod = mod_ref[...]
    hn = _rms_mod(x, ng_ref[...], mod[:, :, 4 * d:5 * d], mod[:, :, 3 * d:4 * d]).reshape(m, d).astype(BF16)
    acc = jnp.zeros((m, d), F32)
    for j in range(D_FF // FFN_COLS):
        lo = j * FFN_COLS
        gate = jnp.dot(hn, w1_ref[:, lo:lo + FFN_COLS], preferred_element_type=F32)
        up = jnp.dot(hn, w1_ref[:, D_FF + lo:D_FF + lo + FFN_COLS], preferred_element_type=F32)
        act = (_silu(gate) * up).astype(BF16)
        acc = acc + jnp.dot(act, w2_ref[lo:lo + FFN_COLS, :], preferred_element_type=F32)
    x2 = x + mod[:, :, 5 * d:6 * d] * acc.reshape(nb, tt, d)
    if final:
        x2 = (x2 * lax.rsqrt(jnp.mean(x2 * x2, axis=-1, keepdims=True) + NORM_EPS)) * fg_ref[...][None]
    o_ref[...] = x2


def _ffn_call(x, mod, ng, w1, w2, fg, final):
    bsz, seq, d = x.shape
    nb, tt = (1, 512) if seq >= 512 else (bsz, seq)
    kern = functools.partial(_ffn_kernel, nb=nb, tt=tt, final=final)
    return pl.pallas_call(
        kern,
        grid=(bsz // nb, seq // tt),
        in_specs=[
            pl.BlockSpec((nb, tt, d), lambda b, t: (b, t, 0)),
            pl.BlockSpec((nb, 1, 6 * d), lambda b, t: (b, 0, 0)),
            _const_spec((1, d)),
            _const_spec((d, 2 * D_FF)),
            _const_spec((D_FF, d)),
            _const_spec((1, d)),
        ],
        out_specs=pl.BlockSpec((nb, tt, d), lambda b, t: (b, t, 0)),
        out_shape=jax.ShapeDtypeStruct((bsz, seq, d), F32),
        compiler_params=pltpu.CompilerParams(
            dimension_semantics=("arbitrary", "arbitrary"), vmem_limit_bytes=VMEM_LIMIT_BYTES),
        name="swiglu_ffn",
    )(x, mod, ng, w1, w2, fg)


def _block_diag(blocks):
    n, r, c = blocks.shape
    eye = jnp.eye(n, dtype=blocks.dtype)
    return (eye[:, None, :, None] * blocks[:, :, None, :]).reshape(n * r, n * c)


def _lru_gate_layout(gate_w, gate_b):
    halves = []
    per = LRU_HEADS // 2
    for j in range(2):
        blk = gate_w[j * per:(j + 1) * per]
        halves.append(jnp.concatenate([_block_diag(blk[:, :, :LRU_BLOCK]), _block_diag(blk[:, :, LRU_BLOCK:])], axis=1))
    bias = jnp.stack([gate_b[:, :LRU_BLOCK].reshape(-1), gate_b[:, LRU_BLOCK:].reshape(-1)])
    return jnp.stack(halves).astype(BF16), bias


def _odd_in_layout(w):
    pad = jnp.zeros((w.shape[0], 128 - GLA_RANK - 2 * MLSTM_HEADS), w.dtype)
    return jnp.concatenate([w[:, 0:1024], w[:, 1040:2576], w[:, 1024:1040], w[:, 2576:2584], pad], axis=1).astype(BF16)


def _lane_row(vec, offset):
    return jnp.zeros((1, 128), F32).at[0, offset:offset + vec.shape[0]].set(vec)


def _gla_state_in(s):
    b = s.shape[0]
    return jnp.swapaxes(s.reshape(b, GLA_HEADS // 2, 2 * GLA_DK, GLA_DV), -1, -2)


def _gla_state_out(s):
    b = s.shape[0]
    return jnp.swapaxes(s, -1, -2).reshape(b, GLA_HEADS, GLA_DK, GLA_DV)


def _trunk(x, mods, states, P, pos0):
    bsz = x.shape[0]
    lru_h, lru_conv, ret, gla, ml_c, ml_n, ml_m, ml_conv = states
    x, h_new, conv_new, ret_new = _even_mixer_call(
        x, mods[0], P['norm_mix_g'][0:1], P['a_w_in'], P['a_w_out'], P['lru_conv_w'], P['lru_conv_b'],
        P['lru_wg'], P['lru_gb'], P['lru_lambda'], P['ret_gn_g'], pos0,
        lru_h.reshape(bsz, 1, LRU_WIDTH), lru_conv, ret)
    x = _ffn_call(x, mods[0], P['norm_ffn_g'][0:1], P['w_ffn_in'][0], P['w_ffn_out'][0], P['final_norm_g'], False)
    m_in = jnp.pad(ml_m, ((0, 0), (0, 128 - MLSTM_HEADS))).reshape(bsz, 1, 128)
    x, gla_new, c_new, n_new, m_new, mconv_new = _odd_mixer_call(
        x, mods[1], P['norm_mix_g'][1:2], P['c_w_in'], P['c_w_out'], P['gla_w_gate'], P['gla_b_gate'],
        P['gla_gn_g'], P['mlstm_conv_w'], P['mlstm_conv_b'], P['mlstm_w_qk'], P['mlstm_w_v'],
        P['mlstm_b_i'], P['mlstm_b_f'], P['mlstm_gn_g'],
        _gla_state_in(gla), ml_c, ml_n, m_in, ml_conv)
    y = _ffn_call(x, mods[1], P['norm_ffn_g'][1:2], P['w_ffn_in'][1], P['w_ffn_out'][1], P['final_norm_g'], True)
    new_states = (h_new.reshape(1, bsz, LRU_WIDTH), conv_new[None], ret_new[None], _gla_state_out(gla_new)[None],
                  c_new[None], n_new[None], m_new[:, 0, :MLSTM_HEADS][None], mconv_new[None])
    return y, new_states


def kernel(x_prompt, x_sample, c_prompt, c_sample, state_lru_h, state_lru_conv, state_ret, state_gla, state_mlstm_C, state_mlstm_n, state_mlstm_m, state_mlstm_conv, w_mod, b_mod, norm_mix_g, norm_ffn_g, w_ffn_in, w_ffn_out, final_norm_g, a_w_in, a_w_out, lru_conv_w, lru_conv_b, lru_gate_w, lru_gate_b, lru_lambda, ret_gn_g, c_w_in, c_w_out, gla_w_gate, gla_b_gate, gla_gn_g, mlstm_conv_w, mlstm_conv_b, mlstm_w_qk, mlstm_w_v, mlstm_b_if, mlstm_gn_g):
    bp = x_prompt.shape[0]
    bs = x_sample.shape[0]
    lru_wg, lru_gb = _lru_gate_layout(lru_gate_w[0], lru_gate_b[0])
    w_gate_pad = jnp.zeros((128, GLA_HEADS * GLA_DK), F32).at[0:GLA_RANK].set(gla_w_gate[0]).astype(BF16)
    P = {
        'norm_mix_g': norm_mix_g, 'norm_ffn_g': norm_ffn_g, 'final_norm_g': final_norm_g.reshape(1, D_MODEL),
        'w_ffn_in': w_ffn_in.astype(BF16), 'w_ffn_out': w_ffn_out.astype(BF16),
        'a_w_in': a_w_in[0].astype(BF16), 'a_w_out': a_w_out[0].astype(BF16),
        'lru_conv_w': lru_conv_w[0], 'lru_conv_b': lru_conv_b, 'lru_wg': lru_wg, 'lru_gb': lru_gb,
        'lru_lambda': lru_lambda, 'ret_gn_g': ret_gn_g,
        'c_w_in': _odd_in_layout(c_w_in[0]), 'c_w_out': c_w_out[0].astype(BF16),
        'gla_w_gate': w_gate_pad, 'gla_b_gate': gla_b_gate, 'gla_gn_g': gla_gn_g,
        'mlstm_conv_w': mlstm_conv_w[0], 'mlstm_conv_b': mlstm_conv_b,
        'mlstm_w_qk': mlstm_w_qk[0].astype(BF16), 'mlstm_w_v': mlstm_w_v[0].astype(BF16),
        'mlstm_b_i': _lane_row(mlstm_b_if[0, :MLSTM_HEADS], _SM_I),
        'mlstm_b_f': _lane_row(mlstm_b_if[0, MLSTM_HEADS:], _SM_F),
        'mlstm_gn_g': mlstm_gn_g,
    }
    mod = _modulation(jnp.concatenate([c_prompt, c_sample], axis=0), w_mod, b_mod)
    mods_p = [mod[l, :bp].reshape(bp, 1, 6 * D_MODEL) for l in range(2)]
    mods_s = [mod[l, bp:].reshape(bs, 1, 6 * D_MODEL) for l in range(2)]

    zeros = lambda *shape: jnp.zeros(shape, F32)
    wd = MLSTM_HEADS * MLSTM_DH
    states_p = (zeros(bp, LRU_WIDTH), zeros(bp, CONV_W - 1, LRU_WIDTH), zeros(bp, RET_HEADS, RET_D, RET_D),
                zeros(bp, GLA_HEADS, GLA_DK, GLA_DV), zeros(bp, MLSTM_HEADS, MLSTM_DH, MLSTM_DH),
                zeros(bp, MLSTM_HEADS, MLSTM_DH), zeros(bp, MLSTM_HEADS), zeros(bp, CONV_W - 1, wd))
    states_s = (state_lru_h[0], state_lru_conv[0], state_ret[0], state_gla[0], state_mlstm_C[0],
                state_mlstm_n[0], state_mlstm_m[0], state_mlstm_conv[0])
    y_p, sp = _trunk(x_prompt, mods_p, states_p, P, 0)
    y_s, ss = _trunk(x_sample, mods_s, states_s, P, PAST_LEN)
    return (y_p, y_s) + sp + ss
```

```python
import functools

import numpy as np
import jax
import jax.numpy as jnp
from jax import lax
from jax.experimental import pallas as pl
from jax.experimental.pallas import tpu as pltpu

F32 = jnp.float32
BF16 = jnp.bfloat16

D_MODEL = 1024
PAST_LEN = 1024
CHUNK = 64
CONV_W = 4
NORM_EPS = 1e-6
LRU_WIDTH = 512
LRU_HEADS = 8
LRU_BLOCK = 64
LRU_C = 8.0
RET_HEADS = 4
RET_D = 128
ROPE_BASE = 10000.0
GLA_HEADS = 4
GLA_DK = 64
GLA_DV = 128
GLA_RANK = 16
GLA_TAU = 16.0
MLSTM_HEADS = 4
MLSTM_DH = 128
D_FF = 2816
EVEN_IN = 3072
WEIGHT_COL_PAD = 128
ODD_IN_PADDED = 2688
FFN_COLS = 256

VMEM_LIMIT_BYTES = 56 * 1024 * 1024
CONV_PAD = 8


def _dot(a, b):
    return jnp.dot(a.astype(BF16), b.astype(BF16), preferred_element_type=F32)


def _dot_nt(a, b):
    return lax.dot_general(a.astype(BF16), b.astype(BF16), (((1,), (1,)), ((), ())),
                           preferred_element_type=F32)


def _dot_tn(a, b):
    return lax.dot_general(a.astype(BF16), b.astype(BF16), (((0,), (0,)), ((), ())),
                           preferred_element_type=F32)


def _sigmoid(x):
    return jax.nn.sigmoid(x)


def _silu(x):
    return x * jax.nn.sigmoid(x)


def _softplus(x):
    return jnp.maximum(x, 0.0) + jnp.log1p(jnp.exp(-jnp.abs(x)))


def _log_sigmoid(x):
    return -_softplus(-x)


def _gelu_tanh(x):
    c = np.sqrt(2.0 / np.pi).astype(np.float32)
    return 0.5 * x * (1.0 + jnp.tanh(c * (x + 0.044715 * (x * x * x))))


def _rms_mod(x, gain, scale, shift):
    y = x * lax.rsqrt(jnp.mean(x * x, axis=-1, keepdims=True) + NORM_EPS)
    return (y * gain[None]) * (1.0 + scale) + shift


def _head_norm(o, gain):
    oc = o - jnp.mean(o, axis=-1, keepdims=True)
    y = oc * lax.rsqrt(jnp.mean(oc * oc, axis=-1, keepdims=True) + NORM_EPS)
    return y * gain


def _row_pos(shape, period):
    return jnp.bitwise_and(lax.broadcasted_iota(jnp.int32, shape, 0), period - 1)


def _seg_cumsum(x, length):
    m, w = x.shape
    x = x.reshape(m // 8, 8, w)
    pos8 = lax.broadcasted_iota(jnp.int32, (1, 8, w), 1)
    for d in (1, 2, 4):
        x = x + jnp.where(pos8 >= d, pltpu.roll(x, d, 1), 0.0)
    out = []
    for g in range(m // 8):
        rows = x[g] if (g * 8) % length == 0 else x[g] + total
        out.append(rows)
        total = rows[7:8]
    return jnp.concatenate(out, axis=0)


def _linear_scan(a, b, h0_rows, seg):
    m, w = a.shape
    a = a.reshape(m // 8, 8, w)
    b = b.reshape(m // 8, 8, w)
    pos = lax.broadcasted_iota(jnp.int32, (1, 8, w), 1)
    for d in (1, 2, 4):
        valid = pos >= d
        b = b + jnp.where(valid, a * pltpu.roll(b, d, 1), 0.0)
        a = a * jnp.where(valid, pltpu.roll(a, d, 1), 1.0)
    a = a.reshape(m, w)
    b = b.reshape(m, w)
    out = []
    for g in range(m // 8):
        if (g * 8) % seg == 0:
            h = h0_rows[(g * 8) // seg]
        rows = b[g * 8:g * 8 + 8] + a[g * 8:g * 8 + 8] * h
        out.append(rows)
        h = rows[7:8]
    return jnp.concatenate(out, axis=0)


def _conv_taps(cbuf, xin, w_ref, b_ref, nb, tt):
    width = xin.shape[-1]
    groups = tt // 8
    cbuf[:, CONV_PAD:CONV_PAD + tt, :] = xin.reshape(nb, tt, width)
    xe = cbuf[...].reshape(nb, groups + 1, 8, width)
    pos = lax.broadcasted_iota(jnp.int32, (1, 1, 8, width), 2)
    out = b_ref[...][None, None] + w_ref[CONV_W - 1:CONV_W, :][None, None] * xe[:, 1:]
    for shift in range(1, CONV_W):
        r = pltpu.roll(xe, shift, 2)
        delayed = jnp.where(pos >= shift, r[:, 1:], r[:, :-1])
        out = out + w_ref[CONV_W - 1 - shift:CONV_W - shift, :][None, None] * delayed
    base = CONV_PAD - (CONV_W - 1)
    tail = cbuf[:, base + tt:CONV_PAD + tt, :]
    cbuf[:, base:CONV_PAD, :] = tail
    return out.reshape(nb * tt, width), tail


def _mod_kernel(c_ref, w_ref, b_ref, o_ref):
    c = c_ref[...]
    o_ref[0] = _dot(_silu(c), w_ref[0]) + b_ref[0]


def _modulation(c_all, w_mod, b_mod):
    depth, d, n = w_mod.shape
    rows = c_all.shape[0]
    tn = 1024
    return pl.pallas_call(
        _mod_kernel,
        grid=(depth, n // tn),
        in_specs=[
            pl.BlockSpec((rows, d), lambda l, j: (0, 0)),
            pl.BlockSpec((1, d, tn), lambda l, j: (l, 0, j)),
            pl.BlockSpec((1, 1, tn), lambda l, j: (l, 0, j)),
        ],
        out_specs=pl.BlockSpec((1, rows, tn), lambda l, j: (l, 0, j)),
        out_shape=jax.ShapeDtypeStruct((depth, rows, n), F32),
        compiler_params=pltpu.CompilerParams(
            dimension_semantics=("arbitrary", "arbitrary"), vmem_limit_bytes=VMEM_LIMIT_BYTES),
        name="adaln_modulation",
    )(c_all, w_mod, b_mod.reshape(depth, 1, n))


def _even_kernel(x_ref, mod_ref, ng_ref, win_ref, wout_ref, cw_ref, cb_ref, wg_ref, gb_ref, lam_ref,
                 gn_ref, cos_ref, sin_ref, dec_ref, qdec_ref, kdec_ref, *rest, nb, tt, sdec, zero_state):
    if zero_state:
        y_ref, h_ref, c_ref, s_ref, cbuf = rest
    else:
        h0_ref, c0_ref, s0_ref, y_ref, h_ref, c_ref, s_ref, cbuf = rest
    d = D_MODEL
    m = nb * tt
    w = LRU_WIDTH
    q0, k0, v0, g0 = 2 * w, 3 * w, 4 * w, 5 * w

    @pl.when(pl.program_id(1) == 0)
    def _():
        cbuf[:, 0:CONV_PAD, :] = jnp.zeros((nb, CONV_PAD, w), F32)
        if zero_state:
            h_ref[...] = jnp.zeros(h_ref.shape, F32)
            s_ref[...] = jnp.zeros(s_ref.shape, F32)
        else:
            h_ref[...] = h0_ref[...]
            s_ref[...] = s0_ref[...]
            cbuf[:, CONV_PAD - (CONV_W - 1):CONV_PAD, :] = c0_ref[...]

    x = x_ref[...]
    mod = mod_ref[...]
    hn = _rms_mod(x, ng_ref[...], mod[:, :, d:2 * d], mod[:, :, 0:d]).reshape(m, d).astype(BF16)
    proj = lambda lo, width: jnp.dot(hn, win_ref[:, lo:lo + width], preferred_element_type=F32)

    xc, tail = _conv_taps(cbuf, proj(0, w), cw_ref, cb_ref, nb, tt)
    c_ref[...] = tail
    q, k = proj(q0, w), proj(k0, w)
    half = w // 2
    r_parts, i_parts = [], []
    for j in range(2):
        gates = _dot(xc[:, j * half:(j + 1) * half], wg_ref[j])
        r_parts.append(_sigmoid(gates[:, :half] + gb_ref[0:1, j * half:(j + 1) * half]))
        i_parts.append(_sigmoid(gates[:, half:] + gb_ref[1:2, j * half:(j + 1) * half]))
    r = jnp.concatenate(r_parts, axis=1)
    i = jnp.concatenate(i_parts, axis=1)
    log_a = (-LRU_C * r) * _softplus(-lam_ref[...])
    a = jnp.exp(log_a)
    b = jnp.sqrt(-jnp.tanh(log_a) * (a * a + 1.0)) * (i * xc)
    v = proj(v0, w)

    cosv = cos_ref[...]
    sinv = sin_ref[...]
    qr, kr = [], []
    for h in range(RET_HEADS):
        lo = h * RET_D
        qh = q[:, lo:lo + RET_D]
        kh = k[:, lo:lo + RET_D]
        qr.append(qh * cosv + pltpu.roll(qh, RET_D // 2, 1) * sinv)
        kr.append((kh * cosv + pltpu.roll(kh, RET_D // 2, 1) * sinv) * (RET_D ** -0.5))

    hs = _linear_scan(a, b, [h_ref[bi] for bi in range(nb)], tt)
    for bi in range(nb):
        h_ref[bi] = hs[bi * tt + tt - 1:bi * tt + tt, :]

    units = [(h, bi) for h in range(RET_HEADS) for bi in range(nb)]
    att, s_old = {}, {}
    for (h, bi) in units:
        rows = slice(bi * tt, (bi + 1) * tt)
        att[h, bi] = _dot_nt(qr[h][rows], kr[h][rows]) * dec_ref[h]
        s_old[h, bi] = s_ref[bi, h]
    o_parts = {}
    for (h, bi) in units:
        rows = slice(bi * tt, (bi + 1) * tt)
        vh = v[rows, h * RET_D:(h + 1) * RET_D]
        o_parts[h, bi] = _dot(att[h, bi], vh) + _dot(qr[h][rows], s_old[h, bi]) * qdec_ref[h]
        s_ref[bi, h] = sdec[h] * s_old[h, bi] + _dot_tn(kr[h][rows] * kdec_ref[h], vh)
    o_heads = [o_parts[h, 0] if nb == 1 else jnp.concatenate([o_parts[h, bi] for bi in range(nb)], axis=0)
               for h in range(RET_HEADS)]

    gb, g = proj(w, w), proj(g0, w)
    mix = [(_gelu_tanh(gb) * hs).astype(BF16)]
    for h in range(RET_HEADS):
        lo = h * RET_D
        mix.append((_silu(g[:, lo:lo + RET_D]) * _head_norm(o_heads[h], gn_ref[:, lo:lo + RET_D])).astype(BF16))

    y = jnp.dot(jnp.concatenate(mix, axis=1), wout_ref[:, 0:D_MODEL], preferred_element_type=F32)
    y_ref[...] = x + mod[:, :, 2 * d:3 * d] * y.reshape(nb, tt, d)


def _retention_tables(chunk):
    h = np.arange(RET_HEADS, dtype=np.float64)
    log_g = np.log1p(-np.exp2(-5.0 - h))
    idx = np.arange(chunk, dtype=np.float64)
    diff = idx[:, None] - idx[None, :]
    decay = np.where(diff >= 0, np.exp(np.maximum(diff, 0.0)[None] * log_g[:, None, None]), 0.0)
    q_dec = np.exp((idx[None, :] + 1.0) * log_g[:, None])
    k_dec = np.exp((chunk - 1.0 - idx)[None, :] * log_g[:, None])
    s_dec = np.exp(chunk * log_g)
    lanes = np.ones((1, 1, RET_D))
    return (jnp.asarray(decay, F32), jnp.asarray(q_dec[:, :, None] * lanes, F32),
            jnp.asarray(k_dec[:, :, None] * lanes, F32), tuple(float(np.float32(s)) for s in s_dec))


def _rope_tables(pos, reps):
    half = RET_D // 2
    inv = ROPE_BASE ** (-np.arange(half, dtype=np.float64) / half)
    ang = pos.astype(np.float64)[:, None] * inv[None, :]
    cos = np.concatenate([np.cos(ang), np.cos(ang)], axis=-1)
    sin = np.concatenate([-np.sin(ang), np.sin(ang)], axis=-1)
    return jnp.asarray(np.tile(cos, (reps, 1)), F32), jnp.asarray(np.tile(sin, (reps, 1)), F32)


def _const_spec(shape):
    zeros = (0,) * len(shape)
    return pl.BlockSpec(shape, lambda b, t: zeros)


def _layer_spec(shape, layer):
    zeros = (0,) * len(shape)
    return pl.BlockSpec((None,) + shape, lambda b, t: (layer,) + zeros)


def _mod_spec(nb, layer, row0):
    return pl.BlockSpec((None, nb, 1, 6 * D_MODEL), lambda b, t: (layer, b + row0 // nb, 0, 0))


def _tiling(batch, seq):
    if seq >= 256:
        return 1, 256, min(CHUNK, seq)
    return batch, seq, min(CHUNK, seq)


def _even_mixer_call(x, mod4, row0, ng, w_in, w_out, conv_w, conv_b, wg, gb, lam, gn, pos0, states):
    bsz, seq, d = x.shape
    nb, tt, _ = _tiling(bsz, seq)
    chunk = tt
    m = nb * tt
    decay, q_dec, k_dec, s_dec = _retention_tables(chunk)
    cos, sin = _rope_tables(pos0 + np.arange(seq), nb)
    w = LRU_WIDTH
    state_spec = lambda shape: pl.BlockSpec((nb,) + shape, lambda b, t: (b,) + (0,) * len(shape))
    kern = functools.partial(_even_kernel, nb=nb, tt=tt, sdec=s_dec, zero_state=states is None)
    state_shapes = [(1, w), (CONV_W - 1, w), (RET_HEADS, RET_D, RET_D)]
    return pl.pallas_call(
        kern,
        grid=(bsz // nb, seq // tt),
        in_specs=[
            pl.BlockSpec((nb, tt, d), lambda b, t: (b, t, 0)),
            _mod_spec(nb, 0, row0),
            _layer_spec((1, d), 0),
            _const_spec((d, EVEN_IN + WEIGHT_COL_PAD)),
            _const_spec((d, d + WEIGHT_COL_PAD)),
            _const_spec((CONV_W, w)),
            _const_spec((1, w)),
            _const_spec((2, w // 2, w)),
            _const_spec((2, w)),
            _const_spec((1, w)),
            _const_spec((1, RET_HEADS * RET_D)),
            pl.BlockSpec((m, RET_D), lambda b, t: (t, 0)),
            pl.BlockSpec((m, RET_D), lambda b, t: (t, 0)),
            _const_spec((RET_HEADS, chunk, chunk)),
            _const_spec((RET_HEADS, chunk, RET_D)),
            _const_spec((RET_HEADS, chunk, RET_D)),
        ] + ([] if states is None else [state_spec(s) for s in state_shapes]),
        out_specs=[
            pl.BlockSpec((nb, tt, d), lambda b, t: (b, t, 0)),
            state_spec((1, w)),
            state_spec((CONV_W - 1, w)),
            state_spec((RET_HEADS, RET_D, RET_D)),
        ],
        out_shape=[
            jax.ShapeDtypeStruct((bsz, seq, d), F32),
            jax.ShapeDtypeStruct((bsz, 1, w), F32),
            jax.ShapeDtypeStruct((bsz, CONV_W - 1, w), F32),
            jax.ShapeDtypeStruct((bsz, RET_HEADS, RET_D, RET_D), F32),
        ],
        scratch_shapes=[pltpu.VMEM((nb, tt + CONV_PAD, w), F32)],
        compiler_params=pltpu.CompilerParams(
            dimension_semantics=("arbitrary", "arbitrary"), vmem_limit_bytes=VMEM_LIMIT_BYTES),
        name="even_mixer",
    )(x, mod4, ng, w_in, w_out, conv_w, conv_b, wg, gb, lam, gn, cos, sin, decay, q_dec, k_dec,
      *([] if states is None else states))


_GQ, _GK, _GV, _GR, _U, _OP, _SM = 0, 256, 512, 1024, 1536, 2048, 2560
_SM_I, _SM_F = GLA_RANK, GLA_RANK + MLSTM_HEADS


def _odd_kernel(x_ref, mod_ref, ng_ref, win_ref, wout_ref, wgate_ref, bgate_ref, ggn_ref,
                cw_ref, cb_ref, wqk_ref, wv_ref, bi_ref, bf_ref, mgn_ref, *rest, nb, tt, gchunk, mchunk, zero_state):
    if zero_state:
        y_ref, s_ref, c_ref, n_ref, m_ref, u_ref, cbuf = rest
    else:
        s0_ref, c0_ref, n0_ref, m0_ref, u0_ref, y_ref, s_ref, c_ref, n_ref, m_ref, u_ref, cbuf = rest
    d = D_MODEL
    m = nb * tt
    wd = MLSTM_HEADS * MLSTM_DH
    dh = MLSTM_DH

    @pl.when(pl.program_id(1) == 0)
    def _():
        cbuf[:, 0:CONV_PAD, :] = jnp.zeros((nb, CONV_PAD, wd), F32)
        if zero_state:
            for ref in (s_ref, c_ref, n_ref, m_ref):
                ref[...] = jnp.zeros(ref.shape, F32)
        else:
            s_ref[...] = s0_ref[...]
            c_ref[...] = c0_ref[...]
            n_ref[...] = n0_ref[...]
            m_ref[...] = m0_ref[...]
            cbuf[:, CONV_PAD - (CONV_W - 1):CONV_PAD, :] = u0_ref[...]

    x = x_ref[...]
    mod = mod_ref[...]
    hn = _rms_mod(x, ng_ref[...], mod[:, :, d:2 * d], mod[:, :, 0:d]).reshape(m, d).astype(BF16)

    proj = lambda lo, width: jnp.dot(hn, win_ref[:, lo:lo + width], preferred_element_type=F32)

    lane = lax.broadcasted_iota(jnp.int32, (1, 128), 1)

    def tri(n):
        row = lax.broadcasted_iota(jnp.int32, (n, n), 0)
        col = lax.broadcasted_iota(jnp.int32, (n, n), 1)
        return row >= col, row == col

    small = proj(_SM, 128)
    u = proj(_U, wd)
    uc, tail = _conv_taps(cbuf, u, cw_ref, cb_ref, nb, tt)
    u_ref[...] = tail
    uc = _silu(uc)
    gq = proj(_GQ, 256)
    gk = proj(_GK, 256) * (GLA_DK ** -0.5)
    log_alpha = _log_sigmoid(_dot(small, wgate_ref[...]) + bgate_ref[...]) / GLA_TAU
    cum = _seg_cumsum(log_alpha, gchunk)
    mq, mk, mv = [], [], []
    for h in range(MLSTM_HEADS):
        lo = h * dh
        qk = _dot(uc[:, lo:lo + dh], wqk_ref[h])
        mq.append(qk[:, 0:dh])
        mk.append(qk[:, dh:2 * dh] * (dh ** -0.5))
        mv.append(_dot(u[:, lo:lo + dh], wv_ref[h]))
    gv = proj(_GV, GLA_HEADS * GLA_DV)
    ic_all = small + bi_ref[...]
    fc_all = _log_sigmoid(small + bf_ref[...])
    b_all = _seg_cumsum(fc_all, mchunk)

    n_gc = tt // gchunk
    n_mc = tt // mchunk
    g_units = [(bi, c) for bi in range(nb) for c in range(n_gc)]
    m_units = [(bi, c) for bi in range(nb) for c in range(n_mc)]
    g_causal, _ = tri(gchunk)
    m_causal, m_diag = tri(mchunk)
    own = [(lane >= hh * GLA_DK) & (lane < (hh + 1) * GLA_DK) for hh in range(2)]

    mA = {}
    for (bi, c) in m_units:
        rows = slice(bi * tt + c * mchunk, bi * tt + (c + 1) * mchunk)
        for h in range(MLSTM_HEADS):
            b_col = b_all[rows, _SM_F + h:_SM_F + h + 1]
            i_col = ic_all[rows, _SM_I + h:_SM_I + h + 1]
            g_col = i_col - b_col
            g_row = jnp.sum(jnp.where(m_diag, g_col, 0.0), axis=0, keepdims=True)
            dlog = jnp.where(m_causal, b_col + g_row, -jnp.inf)
            mA[bi, c, h] = (b_col, i_col, dlog, jnp.max(dlog, axis=-1, keepdims=True),
                            _dot_nt(mq[h][rows], mk[h][rows]))

    gA = {}
    for (bi, c) in g_units:
        rows = slice(bi * tt + c * gchunk, bi * tt + (c + 1) * gchunk)
        cm = cum[rows]
        last = cm[gchunk - 1:gchunk, :]
        ref = 0.5 * last
        qc, kc = gq[rows], gk[rows]
        gA[bi, c] = (qc * jnp.exp(cm - ref), kc * jnp.exp(ref - cm), qc * jnp.exp(cm), kc * jnp.exp(last - cm),
                     jnp.exp(last))

    gB = {}
    for (bi, c) in g_units:
        rows = slice(bi * tt + c * gchunk, bi * tt + (c + 1) * gchunk)
        q_in, k_in, q_st, k_st, e_last = gA[bi, c]
        for h in range(GLA_HEADS):
            p, hh = divmod(h, 2)
            lanes = slice(p * 128, (p + 1) * 128)
            vh = gv[rows, h * GLA_DV:(h + 1) * GLA_DV]
            att = jnp.where(g_causal, _dot_nt(jnp.where(own[hh], q_in[:, lanes], 0.0), k_in[:, lanes]), 0.0)
            gB[bi, c, h] = (att, _dot_tn(vh, jnp.where(own[hh], k_st[:, lanes], 0.0)), vh)

    mB = {}
    for bi in range(nb):
        m_row = m_ref[bi]
        for c in range(n_mc):
            m_new_row = m_row
            for h in range(MLSTM_HEADS):
                b_col, i_col, dlog, rmax, qk = mA[bi, c, h]
                a_col = b_col + m_row[:, h:h + 1]
                m_t = jnp.maximum(a_col, rmax)
                m_last = m_t[mchunk - 1:mchunk, :]
                w_in = jnp.exp(a_col - m_t)
                w_last = jnp.exp((b_col[mchunk - 1:mchunk, :] - b_col) + i_col - m_last)
                mB[bi, c, h] = (qk * jnp.exp(dlog - m_t), w_in, w_last, jnp.exp(-m_t))
                m_new_row = jnp.where(lane == h, m_last, m_new_row)
            m_row = m_new_row
        m_ref[bi] = m_row

    gC = {}
    for (bi, c) in g_units:
        for h in range(GLA_HEADS):
            att, inc, vh = gB[bi, c, h]
            gC[bi, c, h] = _dot(att, vh)
    mC = {}
    for (bi, c) in m_units:
        rows = slice(bi * tt + c * mchunk, bi * tt + (c + 1) * mchunk)
        for h in range(MLSTM_HEADS):
            qkw, w_in, w_last, e_neg = mB[bi, c, h]
            kw = mk[h][rows] * w_last
            mC[bi, c, h] = (_dot(qkw, mv[h][rows]), jnp.sum(qkw, axis=-1, keepdims=True),
                            _dot_tn(kw, mv[h][rows]), jnp.sum(kw, axis=0, keepdims=True))

    gla_o = [[] for _ in range(GLA_HEADS)]
    ml_o = [[] for _ in range(MLSTM_HEADS)]
    for bi in range(nb):
        states = [s_ref[bi, p] for p in range(GLA_HEADS // 2)]
        for c in range(n_gc):
            q_st, e_last = gA[bi, c][2], gA[bi, c][4]
            for h in range(GLA_HEADS):
                p, hh = divmod(h, 2)
                lanes = slice(p * 128, (p + 1) * 128)
                gla_o[h].append(gC[bi, c, h] + _dot_nt(jnp.where(own[hh], q_st[:, lanes], 0.0), states[p]))
            for p in range(GLA_HEADS // 2):
                states[p] = (e_last[:, p * 128:(p + 1) * 128] * states[p] + gB[bi, c, 2 * p][1]) + gB[bi, c, 2 * p + 1][1]
        for p in range(GLA_HEADS // 2):
            s_ref[bi, p] = states[p]
        for h in range(MLSTM_HEADS):
            c_st = c_ref[bi, h]
            n_st = n_ref[bi, h:h + 1, :]
            for c in range(n_mc):
                rows = slice(bi * tt + c * mchunk, bi * tt + (c + 1) * mchunk)
                qkw, w_in, w_last, e_neg = mB[bi, c, h]
                num1, rsum, c_inc, n_inc = mC[bi, c, h]
                qh = mq[h][rows]
                num = num1 + w_in * _dot(qh, c_st)
                den = rsum + w_in * jnp.sum(qh * n_st, axis=-1, keepdims=True)
                ml_o[h].append(num / jnp.maximum(jnp.abs(den), e_neg))
                w_in_last = w_in[mchunk - 1:mchunk, :]
                c_st = w_in_last * c_st + c_inc
                n_st = w_in_last * n_st + n_inc
            c_ref[bi, h] = c_st
            n_ref[bi, h:h + 1, :] = n_st

    cat = lambda parts: parts[0] if len(parts) == 1 else jnp.concatenate(parts, axis=0)
    gr = proj(_GR, GLA_HEADS * GLA_DV)
    og = proj(_OP, wd)
    mix = []
    for h in range(GLA_HEADS):
        lo = h * GLA_DV
        mix.append((_silu(gr[:, lo:lo + GLA_DV]) * _head_norm(cat(gla_o[h]), ggn_ref[:, lo:lo + GLA_DV])).astype(BF16))
    for h in range(MLSTM_HEADS):
        lo = h * dh
        mix.append((_sigmoid(og[:, lo:lo + dh]) * _head_norm(cat(ml_o[h]), mgn_ref[:, lo:lo + dh])).astype(BF16))

    y = jnp.dot(jnp.concatenate(mix, axis=1), wout_ref[:, 0:D_MODEL], preferred_element_type=F32)
    y_ref[...] = x + mod[:, :, 2 * d:3 * d] * y.reshape(nb, tt, d)


def _odd_mixer_call(x, mod4, row0, ng, w_in, w_out, w_gate, b_gate, ggn, conv_w, conv_b, w_qk, w_v, b_i, b_f, mgn,
                    states):
    bsz, seq, d = x.shape
    nb, tt, chunk = _tiling(bsz, seq)
    m = nb * tt
    wd = MLSTM_HEADS * MLSTM_DH
    dh = MLSTM_DH
    state_spec = lambda shape: pl.BlockSpec((nb,) + shape, lambda b, t: (b,) + (0,) * len(shape))
    state_shapes = [(GLA_HEADS // 2, GLA_DV, 2 * GLA_DK), (MLSTM_HEADS, dh, dh), (MLSTM_HEADS, dh), (1, 128),
                    (CONV_W - 1, wd)]
    kern = functools.partial(_odd_kernel, nb=nb, tt=tt, gchunk=chunk, mchunk=min(2 * chunk, tt),
                             zero_state=states is None)
    return pl.pallas_call(
        kern,
        grid=(bsz // nb, seq // tt),
        in_specs=[
            pl.BlockSpec((nb, tt, d), lambda b, t: (b, t, 0)),
            _mod_spec(nb, 1, row0),
            _layer_spec((1, d), 1),
            _const_spec((d, ODD_IN_PADDED)),
            _const_spec((d, d + WEIGHT_COL_PAD)),
            _const_spec((128, GLA_HEADS * GLA_DK)),
            _const_spec((1, GLA_HEADS * GLA_DK)),
            _const_spec((1, GLA_HEADS * GLA_DV)),
            _const_spec((CONV_W, wd)),
            _const_spec((1, wd)),
            _const_spec((MLSTM_HEADS, dh, 2 * dh)),
            _const_spec((MLSTM_HEADS, dh, dh)),
            _const_spec((1, 128)),
            _const_spec((1, 128)),
            _const_spec((1, wd)),
        ] + ([] if states is None else [state_spec(s) for s in state_shapes]),
        out_specs=[pl.BlockSpec((nb, tt, d), lambda b, t: (b, t, 0))] + [state_spec(s) for s in state_shapes],
        out_shape=[jax.ShapeDtypeStruct((bsz, seq, d), F32)]
        + [jax.ShapeDtypeStruct((bsz,) + s, F32) for s in state_shapes],
        scratch_shapes=[pltpu.VMEM((nb, tt + CONV_PAD, wd), F32)],
        compiler_params=pltpu.CompilerParams(
            dimension_semantics=("arbitrary", "arbitrary"), vmem_limit_bytes=VMEM_LIMIT_BYTES),
        name="odd_mixer",
    )(x, mod4, ng, w_in, w_out, w_gate, b_gate, ggn, conv_w, conv_b, w_qk, w_v, b_i, b_f, mgn,
      *([] if states is None else states))


def _ffn_kernel(x_ref, mod_ref, ng_ref, w1_ref, w2_ref, fg_ref, o_ref, *, nb, tt, final):
    d = D_MODEL
    m = nb * tt
    x = x_ref[...]
    mod = mod_ref[...]
    hn = _rms_mod(x, ng_ref[...], mod[:, :, 4 * d:5 * d], mod[:, :, 3 * d:4 * d]).reshape(m, d).astype(BF16)
    acc = jnp.zeros((m, d), F32)
    for j in range(D_FF // FFN_COLS):
        lo = j * FFN_COLS
        gate = jnp.dot(hn, w1_ref[:, lo:lo + FFN_COLS], preferred_element_type=F32)
        up = jnp.dot(hn, w1_ref[:, D_FF + lo:D_FF + lo + FFN_COLS], preferred_element_type=F32)
        act = (_silu(gate) * up).astype(BF16)
        acc = acc + jnp.dot(act, w2_ref[lo:lo + FFN_COLS, :], preferred_element_type=F32)
    x2 = x + mod[:, :, 5 * d:6 * d] * acc.reshape(nb, tt, d)
    if final:
        x2 = (x2 * lax.rsqrt(jnp.mean(x2 * x2, axis=-1, keepdims=True) + NORM_EPS)) * fg_ref[...][None]
    o_ref[...] = x2


def _ffn_call(x, mod4, row0, ng, w1, w2, fg, layer, final):
    bsz, seq, d = x.shape
    nb, tt = (1, 512) if seq >= 512 else (bsz, seq)
    kern = functools.partial(_ffn_kernel, nb=nb, tt=tt, final=final)
    return pl.pallas_call(
        kern,
        grid=(bsz // nb, seq // tt),
        in_specs=[
            pl.BlockSpec((nb, tt, d), lambda b, t: (b, t, 0)),
            _mod_spec(nb, layer, row0),
            _layer_spec((1, d), layer),
            _layer_spec((d, 2 * D_FF), layer),
            _layer_spec((D_FF, d), layer),
            _const_spec((1, d)),
        ],
        out_specs=pl.BlockSpec((nb, tt, d), lambda b, t: (b, t, 0)),
        out_shape=jax.ShapeDtypeStruct((bsz, seq, d), F32),
        compiler_params=pltpu.CompilerParams(
            dimension_semantics=("arbitrary", "arbitrary"), vmem_limit_bytes=VMEM_LIMIT_BYTES),
        name="swiglu_ffn",
    )(x, mod4, ng, w1, w2, fg)


def _block_diag(blocks):
    n, r, c = blocks.shape
    eye = jnp.eye(n, dtype=blocks.dtype)
    return (eye[:, None, :, None] * blocks[:, :, None, :]).reshape(n * r, n * c)


def _lru_gate_layout(gate_w, gate_b):
    halves = []
    per = LRU_HEADS // 2
    for j in range(2):
        blk = gate_w[j * per:(j + 1) * per]
        halves.append(jnp.concatenate([_block_diag(blk[:, :, :LRU_BLOCK]), _block_diag(blk[:, :, LRU_BLOCK:])], axis=1))
    bias = jnp.stack([gate_b[:, :LRU_BLOCK].reshape(-1), gate_b[:, LRU_BLOCK:].reshape(-1)])
    return jnp.stack(halves).astype(BF16), bias


def _odd_in_layout(w):
    pad = jnp.zeros((w.shape[0], 128 - GLA_RANK - 2 * MLSTM_HEADS), w.dtype)
    return jnp.concatenate([w[:, 0:1024], w[:, 1040:2576], w[:, 1024:1040], w[:, 2576:2584], pad], axis=1).astype(BF16)


def _pad_cols(w):
    return jnp.pad(w, ((0, 0), (0, WEIGHT_COL_PAD))).astype(BF16)


def _lane_row(vec, offset):
    return jnp.zeros((1, 128), F32).at[0, offset:offset + vec.shape[0]].set(vec)


def _gla_state_in(s):
    b = s.shape[0]
    return jnp.swapaxes(s.reshape(b, GLA_HEADS // 2, 2 * GLA_DK, GLA_DV), -1, -2)


def _gla_state_out(s):
    b = s.shape[0]
    return jnp.swapaxes(s, -1, -2).reshape(b, GLA_HEADS, GLA_DK, GLA_DV)


def _trunk(x, mod4, row0, states, P, pos0):
    bsz = x.shape[0]
    even_states = odd_states = None
    if states is not None:
        lru_h, lru_conv, ret, gla, ml_c, ml_n, ml_m, ml_conv = states
        m_in = jnp.pad(ml_m, ((0, 0), (0, 128 - MLSTM_HEADS))).reshape(bsz, 1, 128)
        even_states = (lru_h.reshape(bsz, 1, LRU_WIDTH), lru_conv, ret)
        odd_states = (_gla_state_in(gla), ml_c, ml_n, m_in, ml_conv)
    x, h_new, conv_new, ret_new = _even_mixer_call(
        x, mod4, row0, P['norm_mix_g'], P['a_w_in'], P['a_w_out'], P['lru_conv_w'], P['lru_conv_b'],
        P['lru_wg'], P['lru_gb'], P['lru_lambda'], P['ret_gn_g'], pos0, even_states)
    x = _ffn_call(x, mod4, row0, P['norm_ffn_g'], P['w_ffn_in'], P['w_ffn_out'], P['final_norm_g'], 0, False)
    x, gla_new, c_new, n_new, m_new, mconv_new = _odd_mixer_call(
        x, mod4, row0, P['norm_mix_g'], P['c_w_in'], P['c_w_out'], P['gla_w_gate'], P['gla_b_gate'],
        P['gla_gn_g'], P['mlstm_conv_w'], P['mlstm_conv_b'], P['mlstm_w_qk'], P['mlstm_w_v'],
        P['mlstm_b_i'], P['mlstm_b_f'], P['mlstm_gn_g'], odd_states)
    y = _ffn_call(x, mod4, row0, P['norm_ffn_g'], P['w_ffn_in'], P['w_ffn_out'], P['final_norm_g'], 1, True)
    new_states = (h_new.reshape(1, bsz, LRU_WIDTH), conv_new[None], ret_new[None], _gla_state_out(gla_new)[None],
                  c_new[None], n_new[None], m_new[:, 0, :MLSTM_HEADS][None], mconv_new[None])
    return y, new_states


def kernel(x_prompt, x_sample, c_prompt, c_sample, state_lru_h, state_lru_conv, state_ret, state_gla, state_mlstm_C, state_mlstm_n, state_mlstm_m, state_mlstm_conv, w_mod, b_mod, norm_mix_g, norm_ffn_g, w_ffn_in, w_ffn_out, final_norm_g, a_w_in, a_w_out, lru_conv_w, lru_conv_b, lru_gate_w, lru_gate_b, lru_lambda, ret_gn_g, c_w_in, c_w_out, gla_w_gate, gla_b_gate, gla_gn_g, mlstm_conv_w, mlstm_conv_b, mlstm_w_qk, mlstm_w_v, mlstm_b_if, mlstm_gn_g):
    bp = x_prompt.shape[0]
    bs = x_sample.shape[0]
    lru_wg, lru_gb = _lru_gate_layout(lru_gate_w[0], lru_gate_b[0])
    w_gate_pad = jnp.zeros((128, GLA_HEADS * GLA_DK), F32).at[0:GLA_RANK].set(gla_w_gate[0]).astype(BF16)
    P = {
        'norm_mix_g': norm_mix_g.reshape(-1, 1, D_MODEL), 'norm_ffn_g': norm_ffn_g.reshape(-1, 1, D_MODEL),
        'final_norm_g': final_norm_g.reshape(1, D_MODEL),
        'w_ffn_in': w_ffn_in.astype(BF16), 'w_ffn_out': w_ffn_out.astype(BF16),
        'a_w_in': _pad_cols(a_w_in[0]), 'a_w_out': _pad_cols(a_w_out[0]),
        'lru_conv_w': lru_conv_w[0], 'lru_conv_b': lru_conv_b, 'lru_wg': lru_wg, 'lru_gb': lru_gb,
        'lru_lambda': lru_lambda, 'ret_gn_g': ret_gn_g,
        'c_w_in': _odd_in_layout(c_w_in[0]), 'c_w_out': _pad_cols(c_w_out[0]),
        'gla_w_gate': w_gate_pad, 'gla_b_gate': gla_b_gate, 'gla_gn_g': gla_gn_g,
        'mlstm_conv_w': mlstm_conv_w[0], 'mlstm_conv_b': mlstm_conv_b,
        'mlstm_w_qk': mlstm_w_qk[0].astype(BF16), 'mlstm_w_v': mlstm_w_v[0].astype(BF16),
        'mlstm_b_i': _lane_row(mlstm_b_if[0, :MLSTM_HEADS], _SM_I),
        'mlstm_b_f': _lane_row(mlstm_b_if[0, MLSTM_HEADS:], _SM_F),
        'mlstm_gn_g': mlstm_gn_g,
    }
    mod = _modulation(jnp.concatenate([c_sample, c_prompt], axis=0), w_mod, b_mod)
    mod4 = mod.reshape(mod.shape[0], bs + bp, 1, 6 * D_MODEL)
    states_s = (state_lru_h[0], state_lru_conv[0], state_ret[0], state_gla[0], state_mlstm_C[0],
                state_mlstm_n[0], state_mlstm_m[0], state_mlstm_conv[0])
    y_p, sp = _trunk(x_prompt, mod4, bs, None, P, 0)
    y_s, ss = _trunk(x_sample, mod4, 0, states_s, P, PAST_LEN)
    return (y_p, y_s) + sp + ss
```

```python
import functools

import numpy as np
import jax
import jax.numpy as jnp
from jax import lax
from jax.experimental import pallas as pl
from jax.experimental.pallas import tpu as pltpu

F32 = jnp.float32
BF16 = jnp.bfloat16

D_MODEL = 1024
PAST_LEN = 1024
CHUNK = 64
CONV_W = 4
NORM_EPS = 1e-6
LRU_WIDTH = 512
LRU_HEADS = 8
LRU_BLOCK = 64
LRU_C = 8.0
RET_HEADS = 4
RET_D = 128
ROPE_BASE = 10000.0
GLA_HEADS = 4
GLA_DK = 64
GLA_DV = 128
GLA_RANK = 16
GLA_TAU = 16.0
MLSTM_HEADS = 4
MLSTM_DH = 128
D_FF = 2816
EVEN_IN = 3072
WEIGHT_COL_PAD = 128
ODD_IN_PADDED = 2688
FFN_COLS = 256

VMEM_LIMIT_BYTES = 56 * 1024 * 1024
CONV_PAD = 8


def _dot(a, b):
    return jnp.dot(a.astype(BF16), b.astype(BF16), preferred_element_type=F32)


def _dot_nt(a, b):
    return lax.dot_general(a.astype(BF16), b.astype(BF16), (((1,), (1,)), ((), ())),
                           preferred_element_type=F32)


def _dot_tn(a, b):
    return lax.dot_general(a.astype(BF16), b.astype(BF16), (((0,), (0,)), ((), ())),
                           preferred_element_type=F32)


def _sigmoid(x):
    return jax.nn.sigmoid(x)


def _silu(x):
    return x * jax.nn.sigmoid(x)


def _softplus(x):
    return jnp.maximum(x, 0.0) + jnp.log1p(jnp.exp(-jnp.abs(x)))


def _log_sigmoid(x):
    return -_softplus(-x)


def _gelu_tanh(x):
    c = np.sqrt(2.0 / np.pi).astype(np.float32)
    return 0.5 * x * (1.0 + jnp.tanh(c * (x + 0.044715 * (x * x * x))))


def _rms_mod(x, gain, scale, shift):
    y = x * lax.rsqrt(jnp.mean(x * x, axis=-1, keepdims=True) + NORM_EPS)
    return (y * gain[None]) * (1.0 + scale) + shift


def _head_norm(o, gain):
    oc = o - jnp.mean(o, axis=-1, keepdims=True)
    y = oc * lax.rsqrt(jnp.mean(oc * oc, axis=-1, keepdims=True) + NORM_EPS)
    return y * gain


def _row_pos(shape, period):
    return jnp.bitwise_and(lax.broadcasted_iota(jnp.int32, shape, 0), period - 1)


def _seg_cumsum(x, length):
    m, w = x.shape
    x = x.reshape(m // 8, 8, w)
    pos8 = lax.broadcasted_iota(jnp.int32, (1, 8, w), 1)
    for d in (1, 2, 4):
        x = x + jnp.where(pos8 >= d, pltpu.roll(x, d, 1), 0.0)
    out = []
    for g in range(m // 8):
        rows = x[g] if (g * 8) % length == 0 else x[g] + total
        out.append(rows)
        total = rows[7:8]
    return jnp.concatenate(out, axis=0)


def _linear_scan(a, b, h0_rows, seg):
    m, w = a.shape
    a = a.reshape(m // 8, 8, w)
    b = b.reshape(m // 8, 8, w)
    pos = lax.broadcasted_iota(jnp.int32, (1, 8, w), 1)
    for d in (1, 2, 4):
        valid = pos >= d
        b = b + jnp.where(valid, a * pltpu.roll(b, d, 1), 0.0)
        a = a * jnp.where(valid, pltpu.roll(a, d, 1), 1.0)
    a = a.reshape(m, w)
    b = b.reshape(m, w)
    out = []
    for g in range(m // 8):
        if (g * 8) % seg == 0:
            h = h0_rows[(g * 8) // seg]
        rows = b[g * 8:g * 8 + 8] + a[g * 8:g * 8 + 8] * h
        out.append(rows)
        h = rows[7:8]
    return jnp.concatenate(out, axis=0)


def _conv_taps(cbuf, xin, w_ref, b_ref, nb, tt):
    width = xin.shape[-1]
    groups = tt // 8
    cbuf[:, CONV_PAD:CONV_PAD + tt, :] = xin.reshape(nb, tt, width)
    xe = cbuf[...].reshape(nb, groups + 1, 8, width)
    pos = lax.broadcasted_iota(jnp.int32, (1, 1, 8, width), 2)
    out = b_ref[...][None, None] + w_ref[CONV_W - 1:CONV_W, :][None, None] * xe[:, 1:]
    for shift in range(1, CONV_W):
        r = pltpu.roll(xe, shift, 2)
        delayed = jnp.where(pos >= shift, r[:, 1:], r[:, :-1])
        out = out + w_ref[CONV_W - 1 - shift:CONV_W - shift, :][None, None] * delayed
    base = CONV_PAD - (CONV_W - 1)
    tail = cbuf[:, base + tt:CONV_PAD + tt, :]
    cbuf[:, base:CONV_PAD, :] = tail
    return out.reshape(nb * tt, width), tail


def _mod_kernel(c_ref, w_ref, b_ref, o_ref):
    c = c_ref[...]
    o_ref[0] = _dot(_silu(c), w_ref[0]) + b_ref[0]


def _modulation(c_all, w_mod, b_mod):
    depth, d, n = w_mod.shape
    rows = c_all.shape[0]
    tn = 1024
    return pl.pallas_call(
        _mod_kernel,
        grid=(depth, n // tn),
        in_specs=[
            pl.BlockSpec((rows, d), lambda l, j: (0, 0)),
            pl.BlockSpec((1, d, tn), lambda l, j: (l, 0, j)),
            pl.BlockSpec((1, 1, tn), lambda l, j: (l, 0, j)),
        ],
        out_specs=pl.BlockSpec((1, rows, tn), lambda l, j: (l, 0, j)),
        out_shape=jax.ShapeDtypeStruct((depth, rows, n), F32),
        compiler_params=pltpu.CompilerParams(
            dimension_semantics=("arbitrary", "arbitrary"), vmem_limit_bytes=VMEM_LIMIT_BYTES),
        name="adaln_modulation",
    )(c_all, w_mod, b_mod.reshape(depth, 1, n))


def _even_kernel(x_ref, mod_ref, ng_ref, win_ref, wout_ref, cw_ref, cb_ref, wg_ref, gb_ref, lam_ref,
                 gn_ref, cos_ref, sin_ref, dec_ref, qdec_ref, kdec_ref, *rest, nb, tt, sdec, zero_state):
    if zero_state:
        y_ref, h_ref, c_ref, s_ref, cbuf = rest
    else:
        h0_ref, c0_ref, s0_ref, y_ref, h_ref, c_ref, s_ref, cbuf = rest
    d = D_MODEL
    m = nb * tt
    w = LRU_WIDTH
    q0, k0, v0, g0 = 2 * w, 3 * w, 4 * w, 5 * w

    @pl.when(pl.program_id(1) == 0)
    def _():
        cbuf[:, 0:CONV_PAD, :] = jnp.zeros((nb, CONV_PAD, w), F32)
        if zero_state:
            h_ref[...] = jnp.zeros(h_ref.shape, F32)
            s_ref[...] = jnp.zeros(s_ref.shape, F32)
        else:
            h_ref[...] = h0_ref[...]
            s_ref[...] = s0_ref[...]
            cbuf[:, CONV_PAD - (CONV_W - 1):CONV_PAD, :] = c0_ref[...]

    x = x_ref[...]
    mod = mod_ref[...]
    hn = _rms_mod(x, ng_ref[...], mod[:, :, d:2 * d], mod[:, :, 0:d]).reshape(m, d).astype(BF16)
    proj = lambda lo, width: jnp.dot(hn, win_ref[:, lo:lo + width], preferred_element_type=F32)

    xc, tail = _conv_taps(cbuf, proj(0, w), cw_ref, cb_ref, nb, tt)
    c_ref[...] = tail
    q, k = proj(q0, w), proj(k0, w)
    half = w // 2
    r_parts, i_parts = [], []
    for j in range(2):
        gates = _dot(xc[:, j * half:(j + 1) * half], wg_ref[j])
        r_parts.append(_sigmoid(gates[:, :half] + gb_ref[0:1, j * half:(j + 1) * half]))
        i_parts.append(_sigmoid(gates[:, half:] + gb_ref[1:2, j * half:(j + 1) * half]))
    r = jnp.concatenate(r_parts, axis=1)
    i = jnp.concatenate(i_parts, axis=1)
    log_a = (-LRU_C * r) * _softplus(-lam_ref[...])
    a = jnp.exp(log_a)
    b = jnp.sqrt(-jnp.tanh(log_a) * (a * a + 1.0)) * (i * xc)
    v = proj(v0, w)

    cosv = cos_ref[...]
    sinv = sin_ref[...]
    qr, kr = [], []
    for h in range(RET_HEADS):
        lo = h * RET_D
        qh = q[:, lo:lo + RET_D]
        kh = k[:, lo:lo + RET_D]
        qr.append(qh * cosv + pltpu.roll(qh, RET_D // 2, 1) * sinv)
        kr.append((kh * cosv + pltpu.roll(kh, RET_D // 2, 1) * sinv) * (RET_D ** -0.5))

    hs = _linear_scan(a, b, [h_ref[bi] for bi in range(nb)], tt)
    for bi in range(nb):
        h_ref[bi] = hs[bi * tt + tt - 1:bi * tt + tt, :]

    units = [(h, bi) for h in range(RET_HEADS) for bi in range(nb)]
    att, s_old = {}, {}
    for (h, bi) in units:
        rows = slice(bi * tt, (bi + 1) * tt)
        att[h, bi] = _dot_nt(qr[h][rows], kr[h][rows]) * dec_ref[h]
        s_old[h, bi] = s_ref[bi, h]
    o_parts = {}
    for (h, bi) in units:
        rows = slice(bi * tt, (bi + 1) * tt)
        vh = v[rows, h * RET_D:(h + 1) * RET_D]
        o_parts[h, bi] = _dot(att[h, bi], vh) + _dot(qr[h][rows], s_old[h, bi]) * qdec_ref[h]
        s_ref[bi, h] = sdec[h] * s_old[h, bi] + _dot_tn(kr[h][rows] * kdec_ref[h], vh)
    o_heads = [o_parts[h, 0] if nb == 1 else jnp.concatenate([o_parts[h, bi] for bi in range(nb)], axis=0)
               for h in range(RET_HEADS)]

    gb, g = proj(w, w), proj(g0, w)
    mix = [(_gelu_tanh(gb) * hs).astype(BF16)]
    for h in range(RET_HEADS):
        lo = h * RET_D
        mix.append((_silu(g[:, lo:lo + RET_D]) * _head_norm(o_heads[h], gn_ref[:, lo:lo + RET_D])).astype(BF16))

    y = jnp.dot(jnp.concatenate(mix, axis=1), wout_ref[:, 0:D_MODEL], preferred_element_type=F32)
    y_ref[...] = x + mod[:, :, 2 * d:3 * d] * y.reshape(nb, tt, d)


def _retention_tables(chunk):
    h = np.arange(RET_HEADS, dtype=np.float64)
    log_g = np.log1p(-np.exp2(-5.0 - h))
    idx = np.arange(chunk, dtype=np.float64)
    diff = idx[:, None] - idx[None, :]
    decay = np.where(diff >= 0, np.exp(np.maximum(diff, 0.0)[None] * log_g[:, None, None]), 0.0)
    q_dec = np.exp((idx[None, :] + 1.0) * log_g[:, None])
    k_dec = np.exp((chunk - 1.0 - idx)[None, :] * log_g[:, None])
    s_dec = np.exp(chunk * log_g)
    lanes = np.ones((1, 1, RET_D))
    return (jnp.asarray(decay, F32), jnp.asarray(q_dec[:, :, None] * lanes, F32),
            jnp.asarray(k_dec[:, :, None] * lanes, F32), tuple(float(np.float32(s)) for s in s_dec))


def _rope_tables(pos, reps):
    half = RET_D // 2
    inv = ROPE_BASE ** (-np.arange(half, dtype=np.float64) / half)
    ang = pos.astype(np.float64)[:, None] * inv[None, :]
    cos = np.concatenate([np.cos(ang), np.cos(ang)], axis=-1)
    sin = np.concatenate([-np.sin(ang), np.sin(ang)], axis=-1)
    return jnp.asarray(np.tile(cos, (reps, 1)), F32), jnp.asarray(np.tile(sin, (reps, 1)), F32)


def _const_spec(shape):
    zeros = (0,) * len(shape)
    return pl.BlockSpec(shape, lambda b, t: zeros)


def _layer_spec(shape, layer):
    zeros = (0,) * len(shape)
    return pl.BlockSpec((None,) + shape, lambda b, t: (layer,) + zeros)


def _mod_spec(nb, layer, row0):
    return pl.BlockSpec((None, nb, 1, 6 * D_MODEL), lambda b, t: (layer, b + row0 // nb, 0, 0))


def _tiling(batch, seq):
    if seq >= 256:
        return 1, 256, min(CHUNK, seq)
    return batch, seq, min(CHUNK, seq)


def _even_mixer_call(x, mod4, row0, ng, w_in, w_out, conv_w, conv_b, wg, gb, lam, gn, pos0, states):
    bsz, seq, d = x.shape
    nb, tt, _ = _tiling(bsz, seq)
    chunk = tt
    m = nb * tt
    decay, q_dec, k_dec, s_dec = _retention_tables(chunk)
    cos, sin = _rope_tables(pos0 + np.arange(seq), nb)
    w = LRU_WIDTH
    state_spec = lambda shape: pl.BlockSpec((nb,) + shape, lambda b, t: (b,) + (0,) * len(shape))
    kern = functools.partial(_even_kernel, nb=nb, tt=tt, sdec=s_dec, zero_state=states is None)
    state_shapes = [(1, w), (CONV_W - 1, w), (RET_HEADS, RET_D, RET_D)]
    return pl.pallas_call(
        kern,
        grid=(bsz // nb, seq // tt),
        in_specs=[
            pl.BlockSpec((nb, tt, d), lambda b, t: (b, t, 0)),
            _mod_spec(nb, 0, row0),
            _layer_spec((1, d), 0),
            _const_spec((d, EVEN_IN + WEIGHT_COL_PAD)),
            _const_spec((d, d + WEIGHT_COL_PAD)),
            _const_spec((CONV_W, w)),
            _const_spec((1, w)),
            _const_spec((2, w // 2, w)),
            _const_spec((2, w)),
            _const_spec((1, w)),
            _const_spec((1, RET_HEADS * RET_D)),
            pl.BlockSpec((m, RET_D), lambda b, t: (t, 0)),
            pl.BlockSpec((m, RET_D), lambda b, t: (t, 0)),
            _const_spec((RET_HEADS, chunk, chunk)),
            _const_spec((RET_HEADS, chunk, RET_D)),
            _const_spec((RET_HEADS, chunk, RET_D)),
        ] + ([] if states is None else [state_spec(s) for s in state_shapes]),
        out_specs=[
            pl.BlockSpec((nb, tt, d), lambda b, t: (b, t, 0)),
            state_spec((1, w)),
            state_spec((CONV_W - 1, w)),
            state_spec((RET_HEADS, RET_D, RET_D)),
        ],
        out_shape=[
            jax.ShapeDtypeStruct((bsz, seq, d), F32),
            jax.ShapeDtypeStruct((bsz, 1, w), F32),
            jax.ShapeDtypeStruct((bsz, CONV_W - 1, w), F32),
            jax.ShapeDtypeStruct((bsz, RET_HEADS, RET_D, RET_D), F32),
        ],
        scratch_shapes=[pltpu.VMEM((nb, tt + CONV_PAD, w), F32)],
        compiler_params=pltpu.CompilerParams(
            dimension_semantics=("arbitrary", "arbitrary"), vmem_limit_bytes=VMEM_LIMIT_BYTES),
        name="even_mixer",
    )(x, mod4, ng, w_in, w_out, conv_w, conv_b, wg, gb, lam, gn, cos, sin, decay, q_dec, k_dec,
      *([] if states is None else states))


_GQ, _GK, _GV, _GR, _U, _OP, _SM = 0, 256, 512, 1024, 1536, 2048, 2560
_SM_I, _SM_F = GLA_RANK, GLA_RANK + MLSTM_HEADS


def _odd_kernel(x_ref, mod_ref, ng_ref, win_ref, wout_ref, wgate_ref, bgate_ref, ggn_ref,
                cw_ref, cb_ref, wqk_ref, wv_ref, bi_ref, bf_ref, mgn_ref, *rest, nb, tt, gchunk, mchunk, zero_state):
    if zero_state:
        y_ref, s_ref, c_ref, n_ref, m_ref, u_ref, cbuf = rest
    else:
        s0_ref, c0_ref, n0_ref, m0_ref, u0_ref, y_ref, s_ref, c_ref, n_ref, m_ref, u_ref, cbuf = rest
    d = D_MODEL
    m = nb * tt
    wd = MLSTM_HEADS * MLSTM_DH
    dh = MLSTM_DH

    @pl.when(pl.program_id(1) == 0)
    def _():
        cbuf[:, 0:CONV_PAD, :] = jnp.zeros((nb, CONV_PAD, wd), F32)
        if zero_state:
            for ref in (s_ref, c_ref, n_ref, m_ref):
                ref[...] = jnp.zeros(ref.shape, F32)
        else:
            s_ref[...] = s0_ref[...]
            c_ref[...] = c0_ref[...]
            n_ref[...] = n0_ref[...]
            m_ref[...] = m0_ref[...]
            cbuf[:, CONV_PAD - (CONV_W - 1):CONV_PAD, :] = u0_ref[...]

    x = x_ref[...]
    mod = mod_ref[...]
    hn = _rms_mod(x, ng_ref[...], mod[:, :, d:2 * d], mod[:, :, 0:d]).reshape(m, d).astype(BF16)

    proj = lambda lo, width: jnp.dot(hn, win_ref[:, lo:lo + width], preferred_element_type=F32)

    lane = lax.broadcasted_iota(jnp.int32, (1, 128), 1)

    def tri(n):
        row = lax.broadcasted_iota(jnp.int32, (n, n), 0)
        col = lax.broadcasted_iota(jnp.int32, (n, n), 1)
        return row >= col, row == col

    small = proj(_SM, 128)
    u = proj(_U, wd)
    uc, tail = _conv_taps(cbuf, u, cw_ref, cb_ref, nb, tt)
    u_ref[...] = tail
    uc = _silu(uc)
    gq = proj(_GQ, 256)
    gk = proj(_GK, 256) * (GLA_DK ** -0.5)
    log_alpha = _log_sigmoid(_dot(small, wgate_ref[...]) + bgate_ref[...]) / GLA_TAU
    cum = _seg_cumsum(log_alpha, gchunk)
    mq, mk, mv = [], [], []
    for h in range(MLSTM_HEADS):
        lo = h * dh
        qk = _dot(uc[:, lo:lo + dh], wqk_ref[h])
        mq.append(qk[:, 0:dh])
        mk.append(qk[:, dh:2 * dh] * (dh ** -0.5))
        mv.append(_dot(u[:, lo:lo + dh], wv_ref[h]))
    gv = proj(_GV, GLA_HEADS * GLA_DV)
    ic_all = small + bi_ref[...]
    fc_all = _log_sigmoid(small + bf_ref[...])
    b_all = _seg_cumsum(fc_all, mchunk)

    n_gc = tt // gchunk
    n_mc = tt // mchunk
    g_units = [(bi, c) for bi in range(nb) for c in range(n_gc)]
    m_units = [(bi, c) for bi in range(nb) for c in range(n_mc)]
    g_causal, _ = tri(gchunk)
    m_causal, m_diag = tri(mchunk)
    own = [(lane >= hh * GLA_DK) & (lane < (hh + 1) * GLA_DK) for hh in range(2)]

    mA = {}
    for (bi, c) in m_units:
        rows = slice(bi * tt + c * mchunk, bi * tt + (c + 1) * mchunk)
        for h in range(MLSTM_HEADS):
            b_col = b_all[rows, _SM_F + h:_SM_F + h + 1]
            i_col = ic_all[rows, _SM_I + h:_SM_I + h + 1]
            g_col = i_col - b_col
            g_row = jnp.sum(jnp.where(m_diag, g_col, 0.0), axis=0, keepdims=True)
            dlog = jnp.where(m_causal, b_col + g_row, -jnp.inf)
            mA[bi, c, h] = (b_col, i_col, dlog, jnp.max(dlog, axis=-1, keepdims=True),
                            _dot_nt(mq[h][rows], mk[h][rows]))

    gA = {}
    for (bi, c) in g_units:
        rows = slice(bi * tt + c * gchunk, bi * tt + (c + 1) * gchunk)
        cm = cum[rows]
        last = cm[gchunk - 1:gchunk, :]
        ref = 0.5 * last
        qc, kc = gq[rows], gk[rows]
        gA[bi, c] = (qc * jnp.exp(cm - ref), kc * jnp.exp(ref - cm), qc * jnp.exp(cm), kc * jnp.exp(last - cm),
                     jnp.exp(last))

    gB = {}
    for (bi, c) in g_units:
        rows = slice(bi * tt + c * gchunk, bi * tt + (c + 1) * gchunk)
        q_in, k_in, q_st, k_st, e_last = gA[bi, c]
        for h in range(GLA_HEADS):
            p, hh = divmod(h, 2)
            lanes = slice(p * 128, (p + 1) * 128)
            vh = gv[rows, h * GLA_DV:(h + 1) * GLA_DV]
            att = jnp.where(g_causal, _dot_nt(jnp.where(own[hh], q_in[:, lanes], 0.0), k_in[:, lanes]), 0.0)
            gB[bi, c, h] = (att, _dot_tn(vh, jnp.where(own[hh], k_st[:, lanes], 0.0)), vh)

    mB = {}
    for bi in range(nb):
        m_row = m_ref[bi]
        for c in range(n_mc):
            m_new_row = m_row
            for h in range(MLSTM_HEADS):
                b_col, i_col, dlog, rmax, qk = mA[bi, c, h]
                a_col = b_col + m_row[:, h:h + 1]
                m_t = jnp.maximum(a_col, rmax)
                m_last = m_t[mchunk - 1:mchunk, :]
                w_in = jnp.exp(a_col - m_t)
                w_last = jnp.exp((b_col[mchunk - 1:mchunk, :] - b_col) + i_col - m_last)
                mB[bi, c, h] = (qk * jnp.exp(dlog - m_t), w_in, w_last, jnp.exp(-m_t))
                m_new_row = jnp.where(lane == h, m_last, m_new_row)
            m_row = m_new_row
        m_ref[bi] = m_row

    gC = {}
    for (bi, c) in g_units:
        for h in range(GLA_HEADS):
            att, inc, vh = gB[bi, c, h]
            gC[bi, c, h] = _dot(att, vh)
    mC = {}
    for (bi, c) in m_units:
        rows = slice(bi * tt + c * mchunk, bi * tt + (c + 1) * mchunk)
        for h in range(MLSTM_HEADS):
            qkw, w_in, w_last, e_neg = mB[bi, c, h]
            kw = mk[h][rows] * w_last
            mC[bi, c, h] = (_dot(qkw, mv[h][rows]), jnp.sum(qkw, axis=-1, keepdims=True),
                            _dot_tn(kw, mv[h][rows]), jnp.sum(kw, axis=0, keepdims=True))

    gla_o = [[] for _ in range(GLA_HEADS)]
    ml_o = [[] for _ in range(MLSTM_HEADS)]
    for bi in range(nb):
        states = [s_ref[bi, p] for p in range(GLA_HEADS // 2)]
        for c in range(n_gc):
            q_st, e_last = gA[bi, c][2], gA[bi, c][4]
            for h in range(GLA_HEADS):
                p, hh = divmod(h, 2)
                lanes = slice(p * 128, (p + 1) * 128)
                gla_o[h].append(gC[bi, c, h] + _dot_nt(jnp.where(own[hh], q_st[:, lanes], 0.0), states[p]))
            for p in range(GLA_HEADS // 2):
                states[p] = (e_last[:, p * 128:(p + 1) * 128] * states[p] + gB[bi, c, 2 * p][1]) + gB[bi, c, 2 * p + 1][1]
        for p in range(GLA_HEADS // 2):
            s_ref[bi, p] = states[p]
        for h in range(MLSTM_HEADS):
            c_st = c_ref[bi, h]
            n_st = n_ref[bi, h:h + 1, :]
            for c in range(n_mc):
                rows = slice(bi * tt + c * mchunk, bi * tt + (c + 1) * mchunk)
                qkw, w_in, w_last, e_neg = mB[bi, c, h]
                num1, rsum, c_inc, n_inc = mC[bi, c, h]
                qh = mq[h][rows]
                num = num1 + w_in * _dot(qh, c_st)
                den = rsum + w_in * jnp.sum(qh * n_st, axis=-1, keepdims=True)
                ml_o[h].append(num / jnp.maximum(jnp.abs(den), e_neg))
                w_in_last = w_in[mchunk - 1:mchunk, :]
                c_st = w_in_last * c_st + c_inc
                n_st = w_in_last * n_st + n_inc
            c_ref[bi, h] = c_st
            n_ref[bi, h:h + 1, :] = n_st

    cat = lambda parts: parts[0] if len(parts) == 1 else jnp.concatenate(parts, axis=0)
    gr = proj(_GR, GLA_HEADS * GLA_DV)
    og = proj(_OP, wd)
    mix = []
    for h in range(GLA_HEADS):
        lo = h * GLA_DV
        mix.append((_silu(gr[:, lo:lo + GLA_DV]) * _head_norm(cat(gla_o[h]), ggn_ref[:, lo:lo + GLA_DV])).astype(BF16))
    for h in range(MLSTM_HEADS):
        lo = h * dh
        mix.append((_sigmoid(og[:, lo:lo + dh]) * _head_norm(cat(ml_o[h]), mgn_ref[:, lo:lo + dh])).astype(BF16))

    y = jnp.dot(jnp.concatenate(mix, axis=1), wout_ref[:, 0:D_MODEL], preferred_element_type=F32)
    y_ref[...] = x + mod[:, :, 2 * d:3 * d] * y.reshape(nb, tt, d)


def _odd_mixer_call(x, mod4, row0, ng, w_in, w_out, w_gate, b_gate, ggn, conv_w, conv_b, w_qk, w_v, b_i, b_f, mgn,
                    states):
    bsz, seq, d = x.shape
    nb, tt, chunk = _tiling(bsz, seq)
    m = nb * tt
    wd = MLSTM_HEADS * MLSTM_DH
    dh = MLSTM_DH
    state_spec = lambda shape: pl.BlockSpec((nb,) + shape, lambda b, t: (b,) + (0,) * len(shape))
    state_shapes = [(GLA_HEADS // 2, GLA_DV, 2 * GLA_DK), (MLSTM_HEADS, dh, dh), (MLSTM_HEADS, dh), (1, 128),
                    (CONV_W - 1, wd)]
    kern = functools.partial(_odd_kernel, nb=nb, tt=tt, gchunk=chunk, mchunk=min(2 * chunk, tt),
                             zero_state=states is None)
    return pl.pallas_call(
        kern,
        grid=(bsz // nb, seq // tt),
        in_specs=[
            pl.BlockSpec((nb, tt, d), lambda b, t: (b, t, 0)),
            _mod_spec(nb, 1, row0),
            _layer_spec((1, d), 1),
            _const_spec((d, ODD_IN_PADDED)),
            _const_spec((d, d + WEIGHT_COL_PAD)),
            _const_spec((128, GLA_HEADS * GLA_DK)),
            _const_spec((1, GLA_HEADS * GLA_DK)),
            _const_spec((1, GLA_HEADS * GLA_DV)),
            _const_spec((CONV_W, wd)),
            _const_spec((1, wd)),
            _const_spec((MLSTM_HEADS, dh, 2 * dh)),
            _const_spec((MLSTM_HEADS, dh, dh)),
            _const_spec((1, 128)),
            _const_spec((1, 128)),
            _const_spec((1, wd)),
        ] + ([] if states is None else [state_spec(s) for s in state_shapes]),
        out_specs=[pl.BlockSpec((nb, tt, d), lambda b, t: (b, t, 0))] + [state_spec(s) for s in state_shapes],
        out_shape=[jax.ShapeDtypeStruct((bsz, seq, d), F32)]
        + [jax.ShapeDtypeStruct((bsz,) + s, F32) for s in state_shapes],
        scratch_shapes=[pltpu.VMEM((nb, tt + CONV_PAD, wd), F32)],
        compiler_params=pltpu.CompilerParams(
            dimension_semantics=("arbitrary", "arbitrary"), vmem_limit_bytes=VMEM_LIMIT_BYTES),
        name="odd_mixer",
    )(x, mod4, ng, w_in, w_out, w_gate, b_gate, ggn, conv_w, conv_b, w_qk, w_v, b_i, b_f, mgn,
      *([] if states is None else states))


def _ffn_norm(x, mod, gain):
    d = D_MODEL
    rows = x.shape[0] * x.shape[1]
    return _rms_mod(x, gain, mod[:, :, 4 * d:5 * d], mod[:, :, 3 * d:4 * d]).reshape(rows, d).astype(BF16)


def _swiglu(load_hn, w1_ref, w2_ref, between=None):
    acc = None
    for j in range(D_FF // FFN_COLS):
        lo = j * FFN_COLS
        hn = load_hn()
        gate = jnp.dot(hn, w1_ref[:, lo:lo + FFN_COLS], preferred_element_type=F32)
        up = jnp.dot(hn, w1_ref[:, D_FF + lo:D_FF + lo + FFN_COLS], preferred_element_type=F32)
        act = (_silu(gate) * up).astype(BF16)
        part = jnp.dot(act, w2_ref[lo:lo + FFN_COLS, :], preferred_element_type=F32)
        acc = part if acc is None else acc + part
        if j == 0 and between is not None:
            between()
    return acc


def _ffn_finish(x, mod, acc, fg_ref, final):
    d = D_MODEL
    x2 = x + mod[:, :, 5 * d:6 * d] * acc.reshape(x.shape)
    if final:
        x2 = (x2 * lax.rsqrt(jnp.mean(x2 * x2, axis=-1, keepdims=True) + NORM_EPS)) * fg_ref[...][None]
    return x2


def _ffn_kernel(xp_ref, xn_ref, xs_ref, modp_ref, modn_ref, mods_ref, ng_ref, w1_ref, w2_ref, fg_ref,
                yp_ref, ys_ref, hn_scr, *, n_tiles, final):
    s = pl.program_id(0)

    @pl.when(s == 0)
    def _():
        hn_scr[0] = _ffn_norm(xp_ref[...], modp_ref[...], ng_ref[...])

    @pl.when(s < n_tiles)
    def _():
        slot = lax.rem(s, 2)

        def prepare_next():
            hn_scr[1 - slot] = _ffn_norm(xn_ref[...], modn_ref[...], ng_ref[...])

        acc = _swiglu(lambda: hn_scr[slot], w1_ref, w2_ref, prepare_next)
        yp_ref[...] = _ffn_finish(xp_ref[...], modp_ref[...], acc, fg_ref, final)

    @pl.when(s == n_tiles)
    def _():
        xs = xs_ref[...]
        mods = mods_ref[...]
        hn_s = _ffn_norm(xs, mods, ng_ref[...])
        acc = _swiglu(lambda: hn_s, w1_ref, w2_ref)
        ys_ref[...] = _ffn_finish(xs, mods, acc, fg_ref, final)


FFN_ROWS = 512


def _ffn_call(xp, xs, mod4, rowp, rows_, ng, w1, w2, fg, layer, final):
    bp, tp, d = xp.shape
    bs, ts, _ = xs.shape
    per_b = tp // FFN_ROWS
    n_tiles = bp * per_b

    def tile(shift):
        def index(s):
            c = jnp.minimum(s + shift, n_tiles - 1)
            return c // per_b, c % per_b, 0
        return pl.BlockSpec((1, FFN_ROWS, d), index)

    def mod_p(shift):
        def index(s):
            c = jnp.minimum(s + shift, n_tiles - 1)
            return layer, rowp + c // per_b, 0, 0
        return pl.BlockSpec((None, 1, 1, 6 * d), index)

    const = lambda shape: pl.BlockSpec(shape, lambda s: (0,) * len(shape))
    layer_slab = lambda shape: pl.BlockSpec((None,) + shape, lambda s: (layer,) + (0,) * len(shape))
    kern = functools.partial(_ffn_kernel, n_tiles=n_tiles, final=final)
    return pl.pallas_call(
        kern,
        grid=(n_tiles + 1,),
        in_specs=[
            tile(0), tile(1), const((bs, ts, d)),
            mod_p(0), mod_p(1),
            pl.BlockSpec((None, bs, 1, 6 * d), lambda s: (layer, rows_ // bs, 0, 0)),
            layer_slab((1, d)), layer_slab((d, 2 * D_FF)), layer_slab((D_FF, d)), const((1, d)),
        ],
        out_specs=[tile(0), const((bs, ts, d))],
        out_shape=[jax.ShapeDtypeStruct((bp, tp, d), F32), jax.ShapeDtypeStruct((bs, ts, d), F32)],
        scratch_shapes=[pltpu.VMEM((2, FFN_ROWS, d), BF16)],
        compiler_params=pltpu.CompilerParams(
            dimension_semantics=("arbitrary",), vmem_limit_bytes=VMEM_LIMIT_BYTES),
        name="swiglu_ffn",
    )(xp, xp, xs, mod4, mod4, mod4, ng, w1, w2, fg)


def _block_diag(blocks):
    n, r, c = blocks.shape
    eye = jnp.eye(n, dtype=blocks.dtype)
    return (eye[:, None, :, None] * blocks[:, :, None, :]).reshape(n * r, n * c)


def _lru_gate_layout(gate_w, gate_b):
    halves = []
    per = LRU_HEADS // 2
    for j in range(2):
        blk = gate_w[j * per:(j + 1) * per]
        halves.append(jnp.concatenate([_block_diag(blk[:, :, :LRU_BLOCK]), _block_diag(blk[:, :, LRU_BLOCK:])], axis=1))
    bias = jnp.stack([gate_b[:, :LRU_BLOCK].reshape(-1), gate_b[:, LRU_BLOCK:].reshape(-1)])
    return jnp.stack(halves).astype(BF16), bias


def _odd_in_layout(w):
    pad = jnp.zeros((w.shape[0], 128 - GLA_RANK - 2 * MLSTM_HEADS), w.dtype)
    return jnp.concatenate([w[:, 0:1024], w[:, 1040:2576], w[:, 1024:1040], w[:, 2576:2584], pad], axis=1).astype(BF16)


def _pad_cols(w):
    return jnp.pad(w, ((0, 0), (0, WEIGHT_COL_PAD))).astype(BF16)


def _lane_row(vec, offset):
    return jnp.zeros((1, 128), F32).at[0, offset:offset + vec.shape[0]].set(vec)


def _gla_state_in(s):
    b = s.shape[0]
    return jnp.swapaxes(s.reshape(b, GLA_HEADS // 2, 2 * GLA_DK, GLA_DV), -1, -2)


def _gla_state_out(s):
    b = s.shape[0]
    return jnp.swapaxes(s, -1, -2).reshape(b, GLA_HEADS, GLA_DK, GLA_DV)


def _even_layer(x, mod4, row0, states, P, pos0):
    bsz = x.shape[0]
    even_states = None
    if states is not None:
        lru_h, lru_conv, ret = states
        even_states = (lru_h.reshape(bsz, 1, LRU_WIDTH), lru_conv, ret)
    x, h_new, conv_new, ret_new = _even_mixer_call(
        x, mod4, row0, P['norm_mix_g'], P['a_w_in'], P['a_w_out'], P['lru_conv_w'], P['lru_conv_b'],
        P['lru_wg'], P['lru_gb'], P['lru_lambda'], P['ret_gn_g'], pos0, even_states)
    return x, (h_new.reshape(1, bsz, LRU_WIDTH), conv_new[None], ret_new[None])


def _odd_layer(x, mod4, row0, states, P):
    bsz = x.shape[0]
    odd_states = None
    if states is not None:
        gla, ml_c, ml_n, ml_m, ml_conv = states
        m_in = jnp.pad(ml_m, ((0, 0), (0, 128 - MLSTM_HEADS))).reshape(bsz, 1, 128)
        odd_states = (_gla_state_in(gla), ml_c, ml_n, m_in, ml_conv)
    x, gla_new, c_new, n_new, m_new, mconv_new = _odd_mixer_call(
        x, mod4, row0, P['norm_mix_g'], P['c_w_in'], P['c_w_out'], P['gla_w_gate'], P['gla_b_gate'],
        P['gla_gn_g'], P['mlstm_conv_w'], P['mlstm_conv_b'], P['mlstm_w_qk'], P['mlstm_w_v'],
        P['mlstm_b_i'], P['mlstm_b_f'], P['mlstm_gn_g'], odd_states)
    return x, (_gla_state_out(gla_new)[None], c_new[None], n_new[None], m_new[:, 0, :MLSTM_HEADS][None],
               mconv_new[None])


def kernel(x_prompt, x_sample, c_prompt, c_sample, state_lru_h, state_lru_conv, state_ret, state_gla, state_mlstm_C, state_mlstm_n, state_mlstm_m, state_mlstm_conv, w_mod, b_mod, norm_mix_g, norm_ffn_g, w_ffn_in, w_ffn_out, final_norm_g, a_w_in, a_w_out, lru_conv_w, lru_conv_b, lru_gate_w, lru_gate_b, lru_lambda, ret_gn_g, c_w_in, c_w_out, gla_w_gate, gla_b_gate, gla_gn_g, mlstm_conv_w, mlstm_conv_b, mlstm_w_qk, mlstm_w_v, mlstm_b_if, mlstm_gn_g):
    bp = x_prompt.shape[0]
    bs = x_sample.shape[0]
    lru_wg, lru_gb = _lru_gate_layout(lru_gate_w[0], lru_gate_b[0])
    w_gate_pad = jnp.zeros((128, GLA_HEADS * GLA_DK), F32).at[0:GLA_RANK].set(gla_w_gate[0]).astype(BF16)
    P = {
        'norm_mix_g': norm_mix_g.reshape(-1, 1, D_MODEL), 'norm_ffn_g': norm_ffn_g.reshape(-1, 1, D_MODEL),
        'final_norm_g': final_norm_g.reshape(1, D_MODEL),
        'w_ffn_in': w_ffn_in.astype(BF16), 'w_ffn_out': w_ffn_out.astype(BF16),
        'a_w_in': _pad_cols(a_w_in[0]), 'a_w_out': _pad_cols(a_w_out[0]),
        'lru_conv_w': lru_conv_w[0], 'lru_conv_b': lru_conv_b, 'lru_wg': lru_wg, 'lru_gb': lru_gb,
        'lru_lambda': lru_lambda, 'ret_gn_g': ret_gn_g,
        'c_w_in': _odd_in_layout(c_w_in[0]), 'c_w_out': _pad_cols(c_w_out[0]),
        'gla_w_gate': w_gate_pad, 'gla_b_gate': gla_b_gate, 'gla_gn_g': gla_gn_g,
        'mlstm_conv_w': mlstm_conv_w[0], 'mlstm_conv_b': mlstm_conv_b,
        'mlstm_w_qk': mlstm_w_qk[0].astype(BF16), 'mlstm_w_v': mlstm_w_v[0].astype(BF16),
        'mlstm_b_i': _lane_row(mlstm_b_if[0, :MLSTM_HEADS], _SM_I),
        'mlstm_b_f': _lane_row(mlstm_b_if[0, MLSTM_HEADS:], _SM_F),
        'mlstm_gn_g': mlstm_gn_g,
    }
    mod = _modulation(jnp.concatenate([c_sample, c_prompt], axis=0), w_mod, b_mod)
    mod4 = mod.reshape(mod.shape[0], bs + bp, 1, 6 * D_MODEL)
    ffn = lambda xp, xs, layer, final: _ffn_call(xp, xs, mod4, bs, 0, P['norm_ffn_g'], P['w_ffn_in'], P['w_ffn_out'],
                                                 P['final_norm_g'], layer, final)
    xp, sp_even = _even_layer(x_prompt, mod4, bs, None, P, 0)
    xs, ss_even = _even_layer(x_sample, mod4, 0, (state_lru_h[0], state_lru_conv[0], state_ret[0]), P, PAST_LEN)
    xp, xs = ffn(xp, xs, 0, False)
    xp, sp_odd = _odd_layer(xp, mod4, bs, None, P)
    xs, ss_odd = _odd_layer(xs, mod4, 0, (state_gla[0], state_mlstm_C[0], state_mlstm_n[0], state_mlstm_m[0],
                                          state_mlstm_conv[0]), P)
    y_p, y_s = ffn(xp, xs, 1, True)
    return (y_p, y_s) + sp_even + sp_odd + ss_even + ss_odd
```

```python
import functools

import numpy as np
import jax
import jax.numpy as jnp
from jax import lax
from jax.experimental import pallas as pl
from jax.experimental.pallas import tpu as pltpu

F32 = jnp.float32
BF16 = jnp.bfloat16

D_MODEL = 1024
PAST_LEN = 1024
CHUNK = 64
CONV_W = 4
NORM_EPS = 1e-6
LRU_WIDTH = 512
LRU_HEADS = 8
LRU_BLOCK = 64
LRU_C = 8.0
RET_HEADS = 4
RET_D = 128
ROPE_BASE = 10000.0
GLA_HEADS = 4
GLA_DK = 64
GLA_DV = 128
GLA_RANK = 16
GLA_TAU = 16.0
MLSTM_HEADS = 4
MLSTM_DH = 128
D_FF = 2816
EVEN_IN = 3072
WEIGHT_COL_PAD = 128
ODD_IN_PADDED = 2688
FFN_COLS = 256

VMEM_LIMIT_BYTES = 56 * 1024 * 1024
CONV_PAD = 8


def _dot(a, b):
    return jnp.dot(a.astype(BF16), b.astype(BF16), preferred_element_type=F32)


def _dot_nt(a, b):
    return lax.dot_general(a.astype(BF16), b.astype(BF16), (((1,), (1,)), ((), ())),
                           preferred_element_type=F32)


def _dot_tn(a, b):
    return lax.dot_general(a.astype(BF16), b.astype(BF16), (((0,), (0,)), ((), ())),
                           preferred_element_type=F32)


def _sigmoid(x):
    return jax.nn.sigmoid(x)


def _silu(x):
    return x * jax.nn.sigmoid(x)


def _softplus(x):
    return jnp.maximum(x, 0.0) + jnp.log1p(jnp.exp(-jnp.abs(x)))


def _log_sigmoid(x):
    return -_softplus(-x)


def _gelu_tanh(x):
    c = np.sqrt(2.0 / np.pi).astype(np.float32)
    return 0.5 * x * (1.0 + jnp.tanh(c * (x + 0.044715 * (x * x * x))))


def _rms_mod(x, gain, scale, shift):
    y = x * lax.rsqrt(jnp.mean(x * x, axis=-1, keepdims=True) + NORM_EPS)
    return (y * gain[None]) * (1.0 + scale) + shift


def _head_norm(o, gain):
    oc = o - jnp.mean(o, axis=-1, keepdims=True)
    y = oc * lax.rsqrt(jnp.mean(oc * oc, axis=-1, keepdims=True) + NORM_EPS)
    return y * gain


def _row_pos(shape, period):
    return jnp.bitwise_and(lax.broadcasted_iota(jnp.int32, shape, 0), period - 1)


def _seg_cumsum(x, length):
    m, w = x.shape
    x = x.reshape(m // 8, 8, w)
    pos8 = lax.broadcasted_iota(jnp.int32, (1, 8, w), 1)
    for d in (1, 2, 4):
        x = x + jnp.where(pos8 >= d, pltpu.roll(x, d, 1), 0.0)
    out = []
    for g in range(m // 8):
        rows = x[g] if (g * 8) % length == 0 else x[g] + total
        out.append(rows)
        total = rows[7:8]
    return jnp.concatenate(out, axis=0)


def _linear_scan(a, b, h0_rows, seg):
    m, w = a.shape
    a = a.reshape(m // 8, 8, w)
    b = b.reshape(m // 8, 8, w)
    pos = lax.broadcasted_iota(jnp.int32, (1, 8, w), 1)
    for d in (1, 2, 4):
        valid = pos >= d
        b = b + jnp.where(valid, a * pltpu.roll(b, d, 1), 0.0)
        a = a * jnp.where(valid, pltpu.roll(a, d, 1), 1.0)
    a = a.reshape(m, w)
    b = b.reshape(m, w)
    out = []
    for g in range(m // 8):
        if (g * 8) % seg == 0:
            h = h0_rows[(g * 8) // seg]
        rows = b[g * 8:g * 8 + 8] + a[g * 8:g * 8 + 8] * h
        out.append(rows)
        h = rows[7:8]
    return jnp.concatenate(out, axis=0)


def _conv_taps(cbuf, xin, w_ref, b_ref, nb, tt):
    width = xin.shape[-1]
    groups = tt // 8
    cbuf[:, CONV_PAD:CONV_PAD + tt, :] = xin.reshape(nb, tt, width)
    xe = cbuf[...].reshape(nb, groups + 1, 8, width)
    pos = lax.broadcasted_iota(jnp.int32, (1, 1, 8, width), 2)
    out = b_ref[...][None, None] + w_ref[CONV_W - 1:CONV_W, :][None, None] * xe[:, 1:]
    for shift in range(1, CONV_W):
        r = pltpu.roll(xe, shift, 2)
        delayed = jnp.where(pos >= shift, r[:, 1:], r[:, :-1])
        out = out + w_ref[CONV_W - 1 - shift:CONV_W - shift, :][None, None] * delayed
    base = CONV_PAD - (CONV_W - 1)
    tail = cbuf[:, base + tt:CONV_PAD + tt, :]
    cbuf[:, base:CONV_PAD, :] = tail
    return out.reshape(nb * tt, width), tail


def _mod_kernel(c_ref, w_ref, b_ref, o_ref):
    c = c_ref[...]
    o_ref[0] = _dot(_silu(c), w_ref[0]) + b_ref[0]


def _modulation(c_all, w_mod, b_mod):
    depth, d, n = w_mod.shape
    rows = c_all.shape[0]
    tn = 1024
    return pl.pallas_call(
        _mod_kernel,
        grid=(depth, n // tn),
        in_specs=[
            pl.BlockSpec((rows, d), lambda l, j: (0, 0)),
            pl.BlockSpec((1, d, tn), lambda l, j: (l, 0, j)),
            pl.BlockSpec((1, 1, tn), lambda l, j: (l, 0, j)),
        ],
        out_specs=pl.BlockSpec((1, rows, tn), lambda l, j: (l, 0, j)),
        out_shape=jax.ShapeDtypeStruct((depth, rows, n), F32),
        compiler_params=pltpu.CompilerParams(
            dimension_semantics=("arbitrary", "arbitrary"), vmem_limit_bytes=VMEM_LIMIT_BYTES),
        name="adaln_modulation",
    )(c_all, w_mod, b_mod.reshape(depth, 1, n))


def _even_kernel(x_ref, mod_ref, ng_ref, win_ref, wout_ref, cw_ref, cb_ref, wg_ref, gb_ref, lam_ref,
                 gn_ref, cos_ref, sin_ref, dec_ref, qdec_ref, kdec_ref, *rest, nb, tt, chunk, sdec, zero_state):
    if zero_state:
        y_ref, h_ref, c_ref, s_ref, cbuf = rest
    else:
        h0_ref, c0_ref, s0_ref, y_ref, h_ref, c_ref, s_ref, cbuf = rest
    d = D_MODEL
    m = nb * tt
    w = LRU_WIDTH
    q0, k0, v0, g0 = 2 * w, 3 * w, 4 * w, 5 * w

    @pl.when(pl.program_id(1) == 0)
    def _():
        cbuf[:, 0:CONV_PAD, :] = jnp.zeros((nb, CONV_PAD, w), F32)
        if zero_state:
            h_ref[...] = jnp.zeros(h_ref.shape, F32)
            s_ref[...] = jnp.zeros(s_ref.shape, F32)
        else:
            h_ref[...] = h0_ref[...]
            s_ref[...] = s0_ref[...]
            cbuf[:, CONV_PAD - (CONV_W - 1):CONV_PAD, :] = c0_ref[...]

    x = x_ref[...]
    mod = mod_ref[...]
    hn = _rms_mod(x, ng_ref[...], mod[:, :, d:2 * d], mod[:, :, 0:d]).reshape(m, d).astype(BF16)
    proj = lambda lo, width: jnp.dot(hn, win_ref[:, lo:lo + width], preferred_element_type=F32)

    xc, tail = _conv_taps(cbuf, proj(0, w), cw_ref, cb_ref, nb, tt)
    c_ref[...] = tail
    q, k = proj(q0, w), proj(k0, w)
    half = w // 2
    r_parts, i_parts = [], []
    for j in range(2):
        gates = _dot(xc[:, j * half:(j + 1) * half], wg_ref[j])
        r_parts.append(_sigmoid(gates[:, :half] + gb_ref[0:1, j * half:(j + 1) * half]))
        i_parts.append(_sigmoid(gates[:, half:] + gb_ref[1:2, j * half:(j + 1) * half]))
    r = jnp.concatenate(r_parts, axis=1)
    i = jnp.concatenate(i_parts, axis=1)
    log_a = (-LRU_C * r) * _softplus(-lam_ref[...])
    a = jnp.exp(log_a)
    b = jnp.sqrt(-jnp.tanh(log_a) * (a * a + 1.0)) * (i * xc)
    v = proj(v0, w)

    cosv = cos_ref[...]
    sinv = sin_ref[...]
    qr, kr = [], []
    for h in range(RET_HEADS):
        lo = h * RET_D
        qh = q[:, lo:lo + RET_D]
        kh = k[:, lo:lo + RET_D]
        qr.append(qh * cosv + pltpu.roll(qh, RET_D // 2, 1) * sinv)
        kr.append((kh * cosv + pltpu.roll(kh, RET_D // 2, 1) * sinv) * (RET_D ** -0.5))

    hs = _linear_scan(a, b, [h_ref[bi] for bi in range(nb)], tt)
    for bi in range(nb):
        h_ref[bi] = hs[bi * tt + tt - 1:bi * tt + tt, :]

    n_c = tt // chunk
    units = [(h, bi, c) for h in range(RET_HEADS) for bi in range(nb) for c in range(n_c)]
    att, inc = {}, {}
    for (h, bi, c) in units:
        rows = slice(bi * tt + c * chunk, bi * tt + (c + 1) * chunk)
        vh = v[rows, h * RET_D:(h + 1) * RET_D]
        att[h, bi, c] = _dot(_dot_nt(qr[h][rows], kr[h][rows]) * dec_ref[h], vh)
        inc[h, bi, c] = _dot_tn(kr[h][rows] * kdec_ref[h], vh)
    o_parts = {}
    for h in range(RET_HEADS):
        for bi in range(nb):
            s_cur = s_ref[bi, h]
            for c in range(n_c):
                rows = slice(bi * tt + c * chunk, bi * tt + (c + 1) * chunk)
                o_parts[h, bi, c] = att[h, bi, c] + _dot(qr[h][rows], s_cur) * qdec_ref[h]
                s_cur = sdec[h] * s_cur + inc[h, bi, c]
            s_ref[bi, h] = s_cur
    o_heads = [jnp.concatenate([o_parts[h, bi, c] for bi in range(nb) for c in range(n_c)], axis=0)
               if nb * n_c > 1 else o_parts[h, 0, 0] for h in range(RET_HEADS)]

    gb, g = proj(w, w), proj(g0, w)
    mix = [(_gelu_tanh(gb) * hs).astype(BF16)]
    for h in range(RET_HEADS):
        lo = h * RET_D
        mix.append((_silu(g[:, lo:lo + RET_D]) * _head_norm(o_heads[h], gn_ref[:, lo:lo + RET_D])).astype(BF16))

    y = jnp.dot(jnp.concatenate(mix, axis=1), wout_ref[:, 0:D_MODEL], preferred_element_type=F32)
    y_ref[...] = x + mod[:, :, 2 * d:3 * d] * y.reshape(nb, tt, d)


def _retention_tables(chunk):
    h = np.arange(RET_HEADS, dtype=np.float64)
    log_g = np.log1p(-np.exp2(-5.0 - h))
    idx = np.arange(chunk, dtype=np.float64)
    diff = idx[:, None] - idx[None, :]
    decay = np.where(diff >= 0, np.exp(np.maximum(diff, 0.0)[None] * log_g[:, None, None]), 0.0)
    q_dec = np.exp((idx[None, :] + 1.0) * log_g[:, None])
    k_dec = np.exp((chunk - 1.0 - idx)[None, :] * log_g[:, None])
    s_dec = np.exp(chunk * log_g)
    lanes = np.ones((1, 1, RET_D))
    return (jnp.asarray(decay, F32), jnp.asarray(q_dec[:, :, None] * lanes, F32),
            jnp.asarray(k_dec[:, :, None] * lanes, F32), tuple(float(np.float32(s)) for s in s_dec))


def _rope_tables(pos, reps):
    half = RET_D // 2
    inv = ROPE_BASE ** (-np.arange(half, dtype=np.float64) / half)
    ang = pos.astype(np.float64)[:, None] * inv[None, :]
    cos = np.concatenate([np.cos(ang), np.cos(ang)], axis=-1)
    sin = np.concatenate([-np.sin(ang), np.sin(ang)], axis=-1)
    return jnp.asarray(np.tile(cos, (reps, 1)), F32), jnp.asarray(np.tile(sin, (reps, 1)), F32)


def _const_spec(shape):
    zeros = (0,) * len(shape)
    return pl.BlockSpec(shape, lambda b, t: zeros)


def _layer_spec(shape, layer):
    zeros = (0,) * len(shape)
    return pl.BlockSpec((None,) + shape, lambda b, t: (layer,) + zeros)


def _mod_spec(nb, layer, row0):
    return pl.BlockSpec((None, nb, 1, 6 * D_MODEL), lambda b, t: (layer, b + row0 // nb, 0, 0))


def _tiling(batch, seq):
    if seq >= 512:
        return 1, 512, min(CHUNK, seq)
    return batch, seq, min(CHUNK, seq)


def _even_mixer_call(x, mod4, row0, ng, w_in, w_out, conv_w, conv_b, wg, gb, lam, gn, pos0, states):
    bsz, seq, d = x.shape
    nb, tt, _ = _tiling(bsz, seq)
    chunk = min(tt, 256)
    m = nb * tt
    decay, q_dec, k_dec, s_dec = _retention_tables(chunk)
    cos, sin = _rope_tables(pos0 + np.arange(seq), nb)
    w = LRU_WIDTH
    state_spec = lambda shape: pl.BlockSpec((nb,) + shape, lambda b, t: (b,) + (0,) * len(shape))
    kern = functools.partial(_even_kernel, nb=nb, tt=tt, chunk=chunk, sdec=s_dec, zero_state=states is None)
    state_shapes = [(1, w), (CONV_W - 1, w), (RET_HEADS, RET_D, RET_D)]
    return pl.pallas_call(
        kern,
        grid=(bsz // nb, seq // tt),
        in_specs=[
            pl.BlockSpec((nb, tt, d), lambda b, t: (b, t, 0)),
            _mod_spec(nb, 0, row0),
            _layer_spec((1, d), 0),
            _const_spec((d, EVEN_IN + WEIGHT_COL_PAD)),
            _const_spec((d, d + WEIGHT_COL_PAD)),
            _const_spec((CONV_W, w)),
            _const_spec((1, w)),
            _const_spec((2, w // 2, w)),
            _const_spec((2, w)),
            _const_spec((1, w)),
            _const_spec((1, RET_HEADS * RET_D)),
            pl.BlockSpec((m, RET_D), lambda b, t: (t, 0)),
            pl.BlockSpec((m, RET_D), lambda b, t: (t, 0)),
            _const_spec((RET_HEADS, chunk, chunk)),
            _const_spec((RET_HEADS, chunk, RET_D)),
            _const_spec((RET_HEADS, chunk, RET_D)),
        ] + ([] if states is None else [state_spec(s) for s in state_shapes]),
        out_specs=[
            pl.BlockSpec((nb, tt, d), lambda b, t: (b, t, 0)),
            state_spec((1, w)),
            state_spec((CONV_W - 1, w)),
            state_spec((RET_HEADS, RET_D, RET_D)),
        ],
        out_shape=[
            jax.ShapeDtypeStruct((bsz, seq, d), F32),
            jax.ShapeDtypeStruct((bsz, 1, w), F32),
            jax.ShapeDtypeStruct((bsz, CONV_W - 1, w), F32),
            jax.ShapeDtypeStruct((bsz, RET_HEADS, RET_D, RET_D), F32),
        ],
        scratch_shapes=[pltpu.VMEM((nb, tt + CONV_PAD, w), F32)],
        compiler_params=pltpu.CompilerParams(
            dimension_semantics=("arbitrary", "arbitrary"), vmem_limit_bytes=VMEM_LIMIT_BYTES),
        name="even_mixer",
    )(x, mod4, ng, w_in, w_out, conv_w, conv_b, wg, gb, lam, gn, cos, sin, decay, q_dec, k_dec,
      *([] if states is None else states))


_GQ, _GK, _GV, _GR, _U, _OP, _SM = 0, 256, 512, 1024, 1536, 2048, 2560
_SM_I, _SM_F = GLA_RANK, GLA_RANK + MLSTM_HEADS


def _odd_kernel(x_ref, mod_ref, ng_ref, win_ref, wout_ref, wgate_ref, bgate_ref, ggn_ref,
                cw_ref, cb_ref, wqk_ref, wv_ref, bi_ref, bf_ref, mgn_ref, *rest, nb, tt, gchunk, mchunk, zero_state):
    if zero_state:
        y_ref, s_ref, c_ref, n_ref, m_ref, u_ref, cbuf = rest
    else:
        s0_ref, c0_ref, n0_ref, m0_ref, u0_ref, y_ref, s_ref, c_ref, n_ref, m_ref, u_ref, cbuf = rest
    d = D_MODEL
    m = nb * tt
    wd = MLSTM_HEADS * MLSTM_DH
    dh = MLSTM_DH

    @pl.when(pl.program_id(1) == 0)
    def _():
        cbuf[:, 0:CONV_PAD, :] = jnp.zeros((nb, CONV_PAD, wd), F32)
        if zero_state:
            for ref in (s_ref, c_ref, n_ref, m_ref):
                ref[...] = jnp.zeros(ref.shape, F32)
        else:
            s_ref[...] = s0_ref[...]
            c_ref[...] = c0_ref[...]
            n_ref[...] = n0_ref[...]
            m_ref[...] = m0_ref[...]
            cbuf[:, CONV_PAD - (CONV_W - 1):CONV_PAD, :] = u0_ref[...]

    x = x_ref[...]
    mod = mod_ref[...]
    hn = _rms_mod(x, ng_ref[...], mod[:, :, d:2 * d], mod[:, :, 0:d]).reshape(m, d).astype(BF16)

    proj = lambda lo, width: jnp.dot(hn, win_ref[:, lo:lo + width], preferred_element_type=F32)

    lane = lax.broadcasted_iota(jnp.int32, (1, 128), 1)

    def tri(n):
        row = lax.broadcasted_iota(jnp.int32, (n, n), 0)
        col = lax.broadcasted_iota(jnp.int32, (n, n), 1)
        return row >= col, row == col

    small = proj(_SM, 128)
    u = proj(_U, wd)
    uc, tail = _conv_taps(cbuf, u, cw_ref, cb_ref, nb, tt)
    u_ref[...] = tail
    uc = _silu(uc)
    gq = proj(_GQ, 256)
    gk = proj(_GK, 256) * (GLA_DK ** -0.5)
    log_alpha = _log_sigmoid(_dot(small, wgate_ref[...]) + bgate_ref[...]) / GLA_TAU
    cum = _seg_cumsum(log_alpha, gchunk)
    mq, mk, mv = [], [], []
    for h in range(MLSTM_HEADS):
        lo = h * dh
        qk = _dot(uc[:, lo:lo + dh], wqk_ref[h])
        mq.append(qk[:, 0:dh])
        mk.append(qk[:, dh:2 * dh] * (dh ** -0.5))
        mv.append(_dot(u[:, lo:lo + dh], wv_ref[h]))
    gv = proj(_GV, GLA_HEADS * GLA_DV)
    ic_all = small + bi_ref[...]
    fc_all = _log_sigmoid(small + bf_ref[...])
    b_all = _seg_cumsum(fc_all, mchunk)

    n_gc = tt // gchunk
    n_mc = tt // mchunk
    g_units = [(bi, c) for bi in range(nb) for c in range(n_gc)]
    m_units = [(bi, c) for bi in range(nb) for c in range(n_mc)]
    g_causal, _ = tri(gchunk)
    m_causal, m_diag = tri(mchunk)
    own = [(lane >= hh * GLA_DK) & (lane < (hh + 1) * GLA_DK) for hh in range(2)]

    mA = {}
    for (bi, c) in m_units:
        rows = slice(bi * tt + c * mchunk, bi * tt + (c + 1) * mchunk)
        for h in range(MLSTM_HEADS):
            b_col = b_all[rows, _SM_F + h:_SM_F + h + 1]
            i_col = ic_all[rows, _SM_I + h:_SM_I + h + 1]
            g_col = i_col - b_col
            g_row = jnp.sum(jnp.where(m_diag, g_col, 0.0), axis=0, keepdims=True)
            dlog = jnp.where(m_causal, b_col + g_row, -jnp.inf)
            mA[bi, c, h] = (b_col, i_col, dlog, jnp.max(dlog, axis=-1, keepdims=True),
                            _dot_nt(mq[h][rows], mk[h][rows]))

    gA = {}
    for (bi, c) in g_units:
        rows = slice(bi * tt + c * gchunk, bi * tt + (c + 1) * gchunk)
        cm = cum[rows]
        last = cm[gchunk - 1:gchunk, :]
        ref = 0.5 * last
        qc, kc = gq[rows], gk[rows]
        gA[bi, c] = (qc * jnp.exp(cm - ref), kc * jnp.exp(ref - cm), qc * jnp.exp(cm), kc * jnp.exp(last - cm),
                     jnp.exp(last))

    gB = {}
    for (bi, c) in g_units:
        rows = slice(bi * tt + c * gchunk, bi * tt + (c + 1) * gchunk)
        q_in, k_in, q_st, k_st, e_last = gA[bi, c]
        for h in range(GLA_HEADS):
            p, hh = divmod(h, 2)
            lanes = slice(p * 128, (p + 1) * 128)
            vh = gv[rows, h * GLA_DV:(h + 1) * GLA_DV]
            att = jnp.where(g_causal, _dot_nt(jnp.where(own[hh], q_in[:, lanes], 0.0), k_in[:, lanes]), 0.0)
            gB[bi, c, h] = (att, _dot_tn(vh, jnp.where(own[hh], k_st[:, lanes], 0.0)), vh)

    mB = {}
    for bi in range(nb):
        m_row = m_ref[bi]
        for c in range(n_mc):
            m_new_row = m_row
            for h in range(MLSTM_HEADS):
                b_col, i_col, dlog, rmax, qk = mA[bi, c, h]
                a_col = b_col + m_row[:, h:h + 1]
                m_t = jnp.maximum(a_col, rmax)
                m_last = m_t[mchunk - 1:mchunk, :]
                w_in = jnp.exp(a_col - m_t)
                w_last = jnp.exp((b_col[mchunk - 1:mchunk, :] - b_col) + i_col - m_last)
                mB[bi, c, h] = (qk * jnp.exp(dlog - m_t), w_in, w_last, jnp.exp(-m_t))
                m_new_row = jnp.where(lane == h, m_last, m_new_row)
            m_row = m_new_row
        m_ref[bi] = m_row

    gC = {}
    for (bi, c) in g_units:
        for h in range(GLA_HEADS):
            att, inc, vh = gB[bi, c, h]
            gC[bi, c, h] = _dot(att, vh)
    mC = {}
    for (bi, c) in m_units:
        rows = slice(bi * tt + c * mchunk, bi * tt + (c + 1) * mchunk)
        for h in range(MLSTM_HEADS):
            qkw, w_in, w_last, e_neg = mB[bi, c, h]
            kw = mk[h][rows] * w_last
            mC[bi, c, h] = (_dot(qkw, mv[h][rows]), jnp.sum(qkw, axis=-1, keepdims=True),
                            _dot_tn(kw, mv[h][rows]), jnp.sum(kw, axis=0, keepdims=True))

    gla_o = [[] for _ in range(GLA_HEADS)]
    ml_o = [[] for _ in range(MLSTM_HEADS)]
    for bi in range(nb):
        states = [s_ref[bi, p] for p in range(GLA_HEADS // 2)]
        for c in range(n_gc):
            q_st, e_last = gA[bi, c][2], gA[bi, c][4]
            for h in range(GLA_HEADS):
                p, hh = divmod(h, 2)
                lanes = slice(p * 128, (p + 1) * 128)
                gla_o[h].append(gC[bi, c, h] + _dot_nt(jnp.where(own[hh], q_st[:, lanes], 0.0), states[p]))
            for p in range(GLA_HEADS // 2):
                states[p] = (e_last[:, p * 128:(p + 1) * 128] * states[p] + gB[bi, c, 2 * p][1]) + gB[bi, c, 2 * p + 1][1]
        for p in range(GLA_HEADS // 2):
            s_ref[bi, p] = states[p]
        for h in range(MLSTM_HEADS):
            c_st = c_ref[bi, h]
            n_st = n_ref[bi, h:h + 1, :]
            for c in range(n_mc):
                rows = slice(bi * tt + c * mchunk, bi * tt + (c + 1) * mchunk)
                qkw, w_in, w_last, e_neg = mB[bi, c, h]
                num1, rsum, c_inc, n_inc = mC[bi, c, h]
                qh = mq[h][rows]
                num = num1 + w_in * _dot(qh, c_st)
                den = rsum + w_in * jnp.sum(qh * n_st, axis=-1, keepdims=True)
                ml_o[h].append(num / jnp.maximum(jnp.abs(den), e_neg))
                w_in_last = w_in[mchunk - 1:mchunk, :]
                c_st = w_in_last * c_st + c_inc
                n_st = w_in_last * n_st + n_inc
            c_ref[bi, h] = c_st
            n_ref[bi, h:h + 1, :] = n_st

    cat = lambda parts: parts[0] if len(parts) == 1 else jnp.concatenate(parts, axis=0)
    gr = proj(_GR, GLA_HEADS * GLA_DV)
    og = proj(_OP, wd)
    mix = []
    for h in range(GLA_HEADS):
        lo = h * GLA_DV
        mix.append((_silu(gr[:, lo:lo + GLA_DV]) * _head_norm(cat(gla_o[h]), ggn_ref[:, lo:lo + GLA_DV])).astype(BF16))
    for h in range(MLSTM_HEADS):
        lo = h * dh
        mix.append((_sigmoid(og[:, lo:lo + dh]) * _head_norm(cat(ml_o[h]), mgn_ref[:, lo:lo + dh])).astype(BF16))

    y = jnp.dot(jnp.concatenate(mix, axis=1), wout_ref[:, 0:D_MODEL], preferred_element_type=F32)
    y_ref[...] = x + mod[:, :, 2 * d:3 * d] * y.reshape(nb, tt, d)


def _odd_mixer_call(x, mod4, row0, ng, w_in, w_out, w_gate, b_gate, ggn, conv_w, conv_b, w_qk, w_v, b_i, b_f, mgn,
                    states):
    bsz, seq, d = x.shape
    nb, tt, chunk = _tiling(bsz, seq)
    m = nb * tt
    wd = MLSTM_HEADS * MLSTM_DH
    dh = MLSTM_DH
    state_spec = lambda shape: pl.BlockSpec((nb,) + shape, lambda b, t: (b,) + (0,) * len(shape))
    state_shapes = [(GLA_HEADS // 2, GLA_DV, 2 * GLA_DK), (MLSTM_HEADS, dh, dh), (MLSTM_HEADS, dh), (1, 128),
                    (CONV_W - 1, wd)]
    kern = functools.partial(_odd_kernel, nb=nb, tt=tt, gchunk=chunk, mchunk=min(2 * chunk, tt),
                             zero_state=states is None)
    return pl.pallas_call(
        kern,
        grid=(bsz // nb, seq // tt),
        in_specs=[
            pl.BlockSpec((nb, tt, d), lambda b, t: (b, t, 0)),
            _mod_spec(nb, 1, row0),
            _layer_spec((1, d), 1),
            _const_spec((d, ODD_IN_PADDED)),
            _const_spec((d, d + WEIGHT_COL_PAD)),
            _const_spec((128, GLA_HEADS * GLA_DK)),
            _const_spec((1, GLA_HEADS * GLA_DK)),
            _const_spec((1, GLA_HEADS * GLA_DV)),
            _const_spec((CONV_W, wd)),
            _const_spec((1, wd)),
            _const_spec((MLSTM_HEADS, dh, 2 * dh)),
            _const_spec((MLSTM_HEADS, dh, dh)),
            _const_spec((1, 128)),
            _const_spec((1, 128)),
            _const_spec((1, wd)),
        ] + ([] if states is None else [state_spec(s) for s in state_shapes]),
        out_specs=[pl.BlockSpec((nb, tt, d), lambda b, t: (b, t, 0))] + [state_spec(s) for s in state_shapes],
        out_shape=[jax.ShapeDtypeStruct((bsz, seq, d), F32)]
        + [jax.ShapeDtypeStruct((bsz,) + s, F32) for s in state_shapes],
        scratch_shapes=[pltpu.VMEM((nb, tt + CONV_PAD, wd), F32)],
        compiler_params=pltpu.CompilerParams(
            dimension_semantics=("arbitrary", "arbitrary"), vmem_limit_bytes=VMEM_LIMIT_BYTES),
        name="odd_mixer",
    )(x, mod4, ng, w_in, w_out, w_gate, b_gate, ggn, conv_w, conv_b, w_qk, w_v, b_i, b_f, mgn,
      *([] if states is None else states))


def _ffn_norm(x, mod, gain):
    d = D_MODEL
    rows = x.shape[0] * x.shape[1]
    return _rms_mod(x, gain, mod[:, :, 4 * d:5 * d], mod[:, :, 3 * d:4 * d]).reshape(rows, d).astype(BF16)


def _swiglu(load_hn, w1_ref, w2_ref, between=None):
    acc = None
    for j in range(D_FF // FFN_COLS):
        lo = j * FFN_COLS
        hn = load_hn()
        gate = jnp.dot(hn, w1_ref[:, lo:lo + FFN_COLS], preferred_element_type=F32)
        up = jnp.dot(hn, w1_ref[:, D_FF + lo:D_FF + lo + FFN_COLS], preferred_element_type=F32)
        act = (_silu(gate) * up).astype(BF16)
        part = jnp.dot(act, w2_ref[lo:lo + FFN_COLS, :], preferred_element_type=F32)
        acc = part if acc is None else acc + part
        if j == 0 and between is not None:
            between()
    return acc


def _ffn_finish(x, mod, acc, fg_ref, final):
    d = D_MODEL
    x2 = x + mod[:, :, 5 * d:6 * d] * acc.reshape(x.shape)
    if final:
        x2 = (x2 * lax.rsqrt(jnp.mean(x2 * x2, axis=-1, keepdims=True) + NORM_EPS)) * fg_ref[...][None]
    return x2


def _ffn_kernel(xp_ref, xn_ref, xs_ref, modp_ref, modn_ref, mods_ref, ng_ref, w1_ref, w2_ref, fg_ref,
                yp_ref, ys_ref, hn_scr, *, n_tiles, final):
    s = pl.program_id(0)

    @pl.when(s == 0)
    def _():
        hn_scr[0] = _ffn_norm(xp_ref[...], modp_ref[...], ng_ref[...])

    @pl.when(s < n_tiles)
    def _():
        slot = lax.rem(s, 2)

        def prepare_next():
            hn_scr[1 - slot] = _ffn_norm(xn_ref[...], modn_ref[...], ng_ref[...])

        acc = _swiglu(lambda: hn_scr[slot], w1_ref, w2_ref, prepare_next)
        yp_ref[...] = _ffn_finish(xp_ref[...], modp_ref[...], acc, fg_ref, final)

    @pl.when(s == n_tiles)
    def _():
        xs = xs_ref[...]
        mods = mods_ref[...]
        hn_s = _ffn_norm(xs, mods, ng_ref[...])
        acc = _swiglu(lambda: hn_s, w1_ref, w2_ref)
        ys_ref[...] = _ffn_finish(xs, mods, acc, fg_ref, final)


FFN_ROWS = 512


def _ffn_call(xp, xs, mod4, rowp, rows_, ng, w1, w2, fg, layer, final):
    bp, tp, d = xp.shape
    bs, ts, _ = xs.shape
    per_b = tp // FFN_ROWS
    n_tiles = bp * per_b

    def tile(shift):
        def index(s):
            c = jnp.minimum(s + shift, n_tiles - 1)
            return c // per_b, c % per_b, 0
        return pl.BlockSpec((1, FFN_ROWS, d), index)

    def mod_p(shift):
        def index(s):
            c = jnp.minimum(s + shift, n_tiles - 1)
            return layer, rowp + c // per_b, 0, 0
        return pl.BlockSpec((None, 1, 1, 6 * d), index)

    const = lambda shape: pl.BlockSpec(shape, lambda s: (0,) * len(shape))
    layer_slab = lambda shape: pl.BlockSpec((None,) + shape, lambda s: (layer,) + (0,) * len(shape))
    kern = functools.partial(_ffn_kernel, n_tiles=n_tiles, final=final)
    return pl.pallas_call(
        kern,
        grid=(n_tiles + 1,),
        in_specs=[
            tile(0), tile(1), const((bs, ts, d)),
            mod_p(0), mod_p(1),
            pl.BlockSpec((None, bs, 1, 6 * d), lambda s: (layer, rows_ // bs, 0, 0)),
            layer_slab((1, d)), layer_slab((d, 2 * D_FF)), layer_slab((D_FF, d)), const((1, d)),
        ],
        out_specs=[tile(0), const((bs, ts, d))],
        out_shape=[jax.ShapeDtypeStruct((bp, tp, d), F32), jax.ShapeDtypeStruct((bs, ts, d), F32)],
        scratch_shapes=[pltpu.VMEM((2, FFN_ROWS, d), BF16)],
        compiler_params=pltpu.CompilerParams(
            dimension_semantics=("arbitrary",), vmem_limit_bytes=VMEM_LIMIT_BYTES),
        name="swiglu_ffn",
    )(xp, xp, xs, mod4, mod4, mod4, ng, w1, w2, fg)


def _block_diag(blocks):
    n, r, c = blocks.shape
    eye = jnp.eye(n, dtype=blocks.dtype)
    return (eye[:, None, :, None] * blocks[:, :, None, :]).reshape(n * r, n * c)


def _lru_gate_layout(gate_w, gate_b):
    halves = []
    per = LRU_HEADS // 2
    for j in range(2):
        blk = gate_w[j * per:(j + 1) * per]
        halves.append(jnp.concatenate([_block_diag(blk[:, :, :LRU_BLOCK]), _block_diag(blk[:, :, LRU_BLOCK:])], axis=1))
    bias = jnp.stack([gate_b[:, :LRU_BLOCK].reshape(-1), gate_b[:, LRU_BLOCK:].reshape(-1)])
    return jnp.stack(halves).astype(BF16), bias


def _odd_in_layout(w):
    pad = jnp.zeros((w.shape[0], 128 - GLA_RANK - 2 * MLSTM_HEADS), w.dtype)
    return jnp.concatenate([w[:, 0:1024], w[:, 1040:2576], w[:, 1024:1040], w[:, 2576:2584], pad], axis=1).astype(BF16)


def _pad_cols(w):
    return jnp.pad(w, ((0, 0), (0, WEIGHT_COL_PAD))).astype(BF16)


def _lane_row(vec, offset):
    return jnp.zeros((1, 128), F32).at[0, offset:offset + vec.shape[0]].set(vec)


def _gla_state_in(s):
    b = s.shape[0]
    return jnp.swapaxes(s.reshape(b, GLA_HEADS // 2, 2 * GLA_DK, GLA_DV), -1, -2)


def _gla_state_out(s):
    b = s.shape[0]
    return jnp.swapaxes(s, -1, -2).reshape(b, GLA_HEADS, GLA_DK, GLA_DV)


def _even_layer(x, mod4, row0, states, P, pos0):
    bsz = x.shape[0]
    even_states = None
    if states is not None:
        lru_h, lru_conv, ret = states
        even_states = (lru_h.reshape(bsz, 1, LRU_WIDTH), lru_conv, ret)
    x, h_new, conv_new, ret_new = _even_mixer_call(
        x, mod4, row0, P['norm_mix_g'], P['a_w_in'], P['a_w_out'], P['lru_conv_w'], P['lru_conv_b'],
        P['lru_wg'], P['lru_gb'], P['lru_lambda'], P['ret_gn_g'], pos0, even_states)
    return x, (h_new.reshape(1, bsz, LRU_WIDTH), conv_new[None], ret_new[None])


def _odd_layer(x, mod4, row0, states, P):
    bsz = x.shape[0]
    odd_states = None
    if states is not None:
        gla, ml_c, ml_n, ml_m, ml_conv = states
        m_in = jnp.pad(ml_m, ((0, 0), (0, 128 - MLSTM_HEADS))).reshape(bsz, 1, 128)
        odd_states = (_gla_state_in(gla), ml_c, ml_n, m_in, ml_conv)
    x, gla_new, c_new, n_new, m_new, mconv_new = _odd_mixer_call(
        x, mod4, row0, P['norm_mix_g'], P['c_w_in'], P['c_w_out'], P['gla_w_gate'], P['gla_b_gate'],
        P['gla_gn_g'], P['mlstm_conv_w'], P['mlstm_conv_b'], P['mlstm_w_qk'], P['mlstm_w_v'],
        P['mlstm_b_i'], P['mlstm_b_f'], P['mlstm_gn_g'], odd_states)
    return x, (_gla_state_out(gla_new)[None], c_new[None], n_new[None], m_new[:, 0, :MLSTM_HEADS][None],
               mconv_new[None])


def kernel(x_prompt, x_sample, c_prompt, c_sample, state_lru_h, state_lru_conv, state_ret, state_gla, state_mlstm_C, state_mlstm_n, state_mlstm_m, state_mlstm_conv, w_mod, b_mod, norm_mix_g, norm_ffn_g, w_ffn_in, w_ffn_out, final_norm_g, a_w_in, a_w_out, lru_conv_w, lru_conv_b, lru_gate_w, lru_gate_b, lru_lambda, ret_gn_g, c_w_in, c_w_out, gla_w_gate, gla_b_gate, gla_gn_g, mlstm_conv_w, mlstm_conv_b, mlstm_w_qk, mlstm_w_v, mlstm_b_if, mlstm_gn_g):
    bp = x_prompt.shape[0]
    bs = x_sample.shape[0]
    lru_wg, lru_gb = _lru_gate_layout(lru_gate_w[0], lru_gate_b[0])
    w_gate_pad = jnp.zeros((128, GLA_HEADS * GLA_DK), F32).at[0:GLA_RANK].set(gla_w_gate[0]).astype(BF16)
    P = {
        'norm_mix_g': norm_mix_g.reshape(-1, 1, D_MODEL), 'norm_ffn_g': norm_ffn_g.reshape(-1, 1, D_MODEL),
        'final_norm_g': final_norm_g.reshape(1, D_MODEL),
        'w_ffn_in': w_ffn_in.astype(BF16), 'w_ffn_out': w_ffn_out.astype(BF16),
        'a_w_in': _pad_cols(a_w_in[0]), 'a_w_out': _pad_cols(a_w_out[0]),
        'lru_conv_w': lru_conv_w[0], 'lru_conv_b': lru_conv_b, 'lru_wg': lru_wg, 'lru_gb': lru_gb,
        'lru_lambda': lru_lambda, 'ret_gn_g': ret_gn_g,
        'c_w_in': _odd_in_layout(c_w_in[0]), 'c_w_out': _pad_cols(c_w_out[0]),
        'gla_w_gate': w_gate_pad, 'gla_b_gate': gla_b_gate, 'gla_gn_g': gla_gn_g,
        'mlstm_conv_w': mlstm_conv_w[0], 'mlstm_conv_b': mlstm_conv_b,
        'mlstm_w_qk': mlstm_w_qk[0].astype(BF16), 'mlstm_w_v': mlstm_w_v[0].astype(BF16),
        'mlstm_b_i': _lane_row(mlstm_b_if[0, :MLSTM_HEADS], _SM_I),
        'mlstm_b_f': _lane_row(mlstm_b_if[0, MLSTM_HEADS:], _SM_F),
        'mlstm_gn_g': mlstm_gn_g,
    }
    mod = _modulation(jnp.concatenate([c_sample, c_prompt], axis=0), w_mod, b_mod)
    mod4 = mod.reshape(mod.shape[0], bs + bp, 1, 6 * D_MODEL)
    ffn = lambda xp, xs, layer, final: _ffn_call(xp, xs, mod4, bs, 0, P['norm_ffn_g'], P['w_ffn_in'], P['w_ffn_out'],
                                                 P['final_norm_g'], layer, final)
    xp, sp_even = _even_layer(x_prompt, mod4, bs, None, P, 0)
    xs, ss_even = _even_layer(x_sample, mod4, 0, (state_lru_h[0], state_lru_conv[0], state_ret[0]), P, PAST_LEN)
    xp, xs = ffn(xp, xs, 0, False)
    xp, sp_odd = _odd_layer(xp, mod4, bs, None, P)
    xs, ss_odd = _odd_layer(xs, mod4, 0, (state_gla[0], state_mlstm_C[0], state_mlstm_n[0], state_mlstm_m[0],
                                          state_mlstm_conv[0]), P)
    y_p, y_s = ffn(xp, xs, 1, True)
    return (y_p, y_s) + sp_even + sp_odd + ss_even + ss_odd
```

```python
import functools

import numpy as np
import jax
import jax.numpy as jnp
from jax import lax
from jax.experimental import pallas as pl
from jax.experimental.pallas import tpu as pltpu

F32 = jnp.float32
BF16 = jnp.bfloat16

D_MODEL = 1024
PAST_LEN = 1024
CHUNK = 64
CONV_W = 4
NORM_EPS = 1e-6
LRU_WIDTH = 512
LRU_HEADS = 8
LRU_BLOCK = 64
LRU_C = 8.0
RET_HEADS = 4
RET_D = 128
ROPE_BASE = 10000.0
GLA_HEADS = 4
GLA_DK = 64
GLA_DV = 128
GLA_RANK = 16
GLA_TAU = 16.0
MLSTM_HEADS = 4
MLSTM_DH = 128
D_FF = 2816
EVEN_IN = 3072
WEIGHT_COL_PAD = 128
ODD_IN_PADDED = 2688
FFN_COLS = 256

VMEM_LIMIT_BYTES = 56 * 1024 * 1024
CONV_PAD = 8


def _dot(a, b):
    return jnp.dot(a.astype(BF16), b.astype(BF16), preferred_element_type=F32)


def _dot_nt(a, b):
    return lax.dot_general(a.astype(BF16), b.astype(BF16), (((1,), (1,)), ((), ())),
                           preferred_element_type=F32)


def _dot_tn(a, b):
    return lax.dot_general(a.astype(BF16), b.astype(BF16), (((0,), (0,)), ((), ())),
                           preferred_element_type=F32)


def _sigmoid(x):
    return jax.nn.sigmoid(x)


def _silu(x):
    return x * jax.nn.sigmoid(x)


def _softplus(x):
    return jnp.maximum(x, 0.0) + jnp.log1p(jnp.exp(-jnp.abs(x)))


def _log_sigmoid(x):
    return -_softplus(-x)


def _gelu_tanh(x):
    c = np.sqrt(2.0 / np.pi).astype(np.float32)
    return 0.5 * x * (1.0 + jnp.tanh(c * (x + 0.044715 * (x * x * x))))


def _rms_mod(x, gain, scale, shift):
    y = x * lax.rsqrt(jnp.mean(x * x, axis=-1, keepdims=True) + NORM_EPS)
    return (y * gain[None]) * (1.0 + scale) + shift


def _head_norm(o, gain):
    oc = o - jnp.mean(o, axis=-1, keepdims=True)
    y = oc * lax.rsqrt(jnp.mean(oc * oc, axis=-1, keepdims=True) + NORM_EPS)
    return y * gain


def _row_pos(shape, period):
    return jnp.bitwise_and(lax.broadcasted_iota(jnp.int32, shape, 0), period - 1)


def _seg_cumsum(x, length):
    m, w = x.shape
    x = x.reshape(m // 8, 8, w)
    pos8 = lax.broadcasted_iota(jnp.int32, (1, 8, w), 1)
    for d in (1, 2, 4):
        x = x + jnp.where(pos8 >= d, pltpu.roll(x, d, 1), 0.0)
    out = []
    for g in range(m // 8):
        rows = x[g] if (g * 8) % length == 0 else x[g] + total
        out.append(rows)
        total = rows[7:8]
    return jnp.concatenate(out, axis=0)


def _linear_scan(a, b, h0_rows, seg):
    m, w = a.shape
    a = a.reshape(m // 8, 8, w)
    b = b.reshape(m // 8, 8, w)
    pos = lax.broadcasted_iota(jnp.int32, (1, 8, w), 1)
    for d in (1, 2, 4):
        valid = pos >= d
        b = b + jnp.where(valid, a * pltpu.roll(b, d, 1), 0.0)
        a = a * jnp.where(valid, pltpu.roll(a, d, 1), 1.0)
    a = a.reshape(m, w)
    b = b.reshape(m, w)
    out = []
    for g in range(m // 8):
        if (g * 8) % seg == 0:
            h = h0_rows[(g * 8) // seg]
        rows = b[g * 8:g * 8 + 8] + a[g * 8:g * 8 + 8] * h
        out.append(rows)
        h = rows[7:8]
    return jnp.concatenate(out, axis=0)


def _conv_taps(cbuf, xin, w_ref, b_ref, nb, tt):
    width = xin.shape[-1]
    groups = tt // 8
    cbuf[:, CONV_PAD:CONV_PAD + tt, :] = xin.reshape(nb, tt, width)
    xe = cbuf[...].reshape(nb, groups + 1, 8, width)
    pos = lax.broadcasted_iota(jnp.int32, (1, 1, 8, width), 2)
    out = b_ref[...][None, None] + w_ref[CONV_W - 1:CONV_W, :][None, None] * xe[:, 1:]
    for shift in range(1, CONV_W):
        r = pltpu.roll(xe, shift, 2)
        delayed = jnp.where(pos >= shift, r[:, 1:], r[:, :-1])
        out = out + w_ref[CONV_W - 1 - shift:CONV_W - shift, :][None, None] * delayed
    base = CONV_PAD - (CONV_W - 1)
    tail = cbuf[:, base + tt:CONV_PAD + tt, :]
    cbuf[:, base:CONV_PAD, :] = tail
    return out.reshape(nb * tt, width), tail


def _mod_kernel(c_ref, w_ref, b_ref, o_ref):
    c = c_ref[...]
    o_ref[0] = _dot(_silu(c), w_ref[0]) + b_ref[0]


def _modulation(c_all, w_mod, b_mod):
    depth, d, n = w_mod.shape
    rows = c_all.shape[0]
    tn = 1024
    return pl.pallas_call(
        _mod_kernel,
        grid=(depth, n // tn),
        in_specs=[
            pl.BlockSpec((rows, d), lambda l, j: (0, 0)),
            pl.BlockSpec((1, d, tn), lambda l, j: (l, 0, j)),
            pl.BlockSpec((1, 1, tn), lambda l, j: (l, 0, j)),
        ],
        out_specs=pl.BlockSpec((1, rows, tn), lambda l, j: (l, 0, j)),
        out_shape=jax.ShapeDtypeStruct((depth, rows, n), F32),
        compiler_params=pltpu.CompilerParams(
            dimension_semantics=("arbitrary", "arbitrary"), vmem_limit_bytes=VMEM_LIMIT_BYTES),
        name="adaln_modulation",
    )(c_all, w_mod, b_mod.reshape(depth, 1, n))


def _even_kernel(x_ref, mod_ref, ng_ref, win_ref, wout_ref, cw_ref, cb_ref, wg_ref, gb_ref, lam_ref,
                 gn_ref, cos_ref, sin_ref, dec_ref, qdec_ref, kdec_ref, *rest, nb, tt, chunk, sdec, zero_state):
    if zero_state:
        y_ref, h_ref, c_ref, s_ref, cbuf = rest
    else:
        h0_ref, c0_ref, s0_ref, y_ref, h_ref, c_ref, s_ref, cbuf = rest
    d = D_MODEL
    m = nb * tt
    w = LRU_WIDTH
    q0, k0, v0, g0 = 2 * w, 3 * w, 4 * w, 5 * w

    @pl.when(pl.program_id(1) == 0)
    def _():
        cbuf[:, 0:CONV_PAD, :] = jnp.zeros((nb, CONV_PAD, w), F32)
        if zero_state:
            h_ref[...] = jnp.zeros(h_ref.shape, F32)
            s_ref[...] = jnp.zeros(s_ref.shape, F32)
        else:
            h_ref[...] = h0_ref[...]
            s_ref[...] = s0_ref[...]
            cbuf[:, CONV_PAD - (CONV_W - 1):CONV_PAD, :] = c0_ref[...]

    x = x_ref[...]
    mod = mod_ref[...]
    hn = _rms_mod(x, ng_ref[...], mod[:, :, d:2 * d], mod[:, :, 0:d]).reshape(m, d).astype(BF16)
    proj = lambda lo, width: jnp.dot(hn, win_ref[:, lo:lo + width], preferred_element_type=F32)

    xc, tail = _conv_taps(cbuf, proj(0, w), cw_ref, cb_ref, nb, tt)
    c_ref[...] = tail
    q, k = proj(q0, w), proj(k0, w)
    half = w // 2
    r_parts, i_parts = [], []
    for j in range(2):
        gates = _dot(xc[:, j * half:(j + 1) * half], wg_ref[j])
        r_parts.append(_sigmoid(gates[:, :half] + gb_ref[0:1, j * half:(j + 1) * half]))
        i_parts.append(_sigmoid(gates[:, half:] + gb_ref[1:2, j * half:(j + 1) * half]))
    r = jnp.concatenate(r_parts, axis=1)
    i = jnp.concatenate(i_parts, axis=1)
    log_a = (-LRU_C * r) * _softplus(-lam_ref[...])
    a = jnp.exp(log_a)
    b = jnp.sqrt(-jnp.tanh(log_a) * (a * a + 1.0)) * (i * xc)
    v = proj(v0, w)

    cosv = cos_ref[...]
    sinv = sin_ref[...]
    qr, kr = [], []
    for h in range(RET_HEADS):
        lo = h * RET_D
        qh = q[:, lo:lo + RET_D]
        kh = k[:, lo:lo + RET_D]
        qr.append(qh * cosv + pltpu.roll(qh, RET_D // 2, 1) * sinv)
        kr.append((kh * cosv + pltpu.roll(kh, RET_D // 2, 1) * sinv) * (RET_D ** -0.5))

    hs = _linear_scan(a, b, [h_ref[bi] for bi in range(nb)], tt)
    for bi in range(nb):
        h_ref[bi] = hs[bi * tt + tt - 1:bi * tt + tt, :]

    n_c = tt // chunk
    units = [(h, bi, c) for h in range(RET_HEADS) for bi in range(nb) for c in range(n_c)]
    att, inc = {}, {}
    for (h, bi, c) in units:
        rows = slice(bi * tt + c * chunk, bi * tt + (c + 1) * chunk)
        vh = v[rows, h * RET_D:(h + 1) * RET_D]
        att[h, bi, c] = _dot(_dot_nt(qr[h][rows], kr[h][rows]) * dec_ref[h], vh)
        inc[h, bi, c] = _dot_tn(kr[h][rows] * kdec_ref[h], vh)
    o_parts = {}
    for h in range(RET_HEADS):
        for bi in range(nb):
            s_cur = s_ref[bi, h]
            for c in range(n_c):
                rows = slice(bi * tt + c * chunk, bi * tt + (c + 1) * chunk)
                o_parts[h, bi, c] = att[h, bi, c] + _dot(qr[h][rows], s_cur) * qdec_ref[h]
                s_cur = sdec[h] * s_cur + inc[h, bi, c]
            s_ref[bi, h] = s_cur
    o_heads = [jnp.concatenate([o_parts[h, bi, c] for bi in range(nb) for c in range(n_c)], axis=0)
               if nb * n_c > 1 else o_parts[h, 0, 0] for h in range(RET_HEADS)]

    gb, g = proj(w, w), proj(g0, w)
    mix = [(_gelu_tanh(gb) * hs).astype(BF16)]
    for h in range(RET_HEADS):
        lo = h * RET_D
        mix.append((_silu(g[:, lo:lo + RET_D]) * _head_norm(o_heads[h], gn_ref[:, lo:lo + RET_D])).astype(BF16))

    y = jnp.dot(jnp.concatenate(mix, axis=1), wout_ref[:, 0:D_MODEL], preferred_element_type=F32)
    y_ref[...] = x + mod[:, :, 2 * d:3 * d] * y.reshape(nb, tt, d)


def _retention_tables(chunk):
    h = np.arange(RET_HEADS, dtype=np.float64)
    log_g = np.log1p(-np.exp2(-5.0 - h))
    idx = np.arange(chunk, dtype=np.float64)
    diff = idx[:, None] - idx[None, :]
    decay = np.where(diff >= 0, np.exp(np.maximum(diff, 0.0)[None] * log_g[:, None, None]), 0.0)
    q_dec = np.exp((idx[None, :] + 1.0) * log_g[:, None])
    k_dec = np.exp((chunk - 1.0 - idx)[None, :] * log_g[:, None])
    s_dec = np.exp(chunk * log_g)
    lanes = np.ones((1, 1, RET_D))
    return (jnp.asarray(decay, F32), jnp.asarray(q_dec[:, :, None] * lanes, F32),
            jnp.asarray(k_dec[:, :, None] * lanes, F32), tuple(float(np.float32(s)) for s in s_dec))


def _rope_tables(pos, reps):
    half = RET_D // 2
    inv = ROPE_BASE ** (-np.arange(half, dtype=np.float64) / half)
    ang = pos.astype(np.float64)[:, None] * inv[None, :]
    cos = np.concatenate([np.cos(ang), np.cos(ang)], axis=-1)
    sin = np.concatenate([-np.sin(ang), np.sin(ang)], axis=-1)
    return jnp.asarray(np.tile(cos, (reps, 1)), F32), jnp.asarray(np.tile(sin, (reps, 1)), F32)


def _const_spec(shape):
    zeros = (0,) * len(shape)
    return pl.BlockSpec(shape, lambda b, t: zeros)


def _layer_spec(shape, layer):
    zeros = (0,) * len(shape)
    return pl.BlockSpec((None,) + shape, lambda b, t: (layer,) + zeros)


def _mod_spec(nb, layer, row0):
    return pl.BlockSpec((None, nb, 1, 6 * D_MODEL), lambda b, t: (layer, b + row0 // nb, 0, 0))


def _tiling(batch, seq):
    if seq >= 1024:
        return 1, 1024, min(CHUNK, seq)
    return batch, seq, min(CHUNK, seq)


def _even_mixer_call(x, mod4, row0, ng, w_in, w_out, conv_w, conv_b, wg, gb, lam, gn, pos0, states):
    bsz, seq, d = x.shape
    nb, tt, _ = _tiling(bsz, seq)
    chunk = min(tt, 256)
    m = nb * tt
    decay, q_dec, k_dec, s_dec = _retention_tables(chunk)
    cos, sin = _rope_tables(pos0 + np.arange(seq), nb)
    w = LRU_WIDTH
    state_spec = lambda shape: pl.BlockSpec((nb,) + shape, lambda b, t: (b,) + (0,) * len(shape))
    kern = functools.partial(_even_kernel, nb=nb, tt=tt, chunk=chunk, sdec=s_dec, zero_state=states is None)
    state_shapes = [(1, w), (CONV_W - 1, w), (RET_HEADS, RET_D, RET_D)]
    return pl.pallas_call(
        kern,
        grid=(bsz // nb, seq // tt),
        in_specs=[
            pl.BlockSpec((nb, tt, d), lambda b, t: (b, t, 0)),
            _mod_spec(nb, 0, row0),
            _layer_spec((1, d), 0),
            _const_spec((d, EVEN_IN + WEIGHT_COL_PAD)),
            _const_spec((d, d + WEIGHT_COL_PAD)),
            _const_spec((CONV_W, w)),
            _const_spec((1, w)),
            _const_spec((2, w // 2, w)),
            _const_spec((2, w)),
            _const_spec((1, w)),
            _const_spec((1, RET_HEADS * RET_D)),
            pl.BlockSpec((m, RET_D), lambda b, t: (t, 0)),
            pl.BlockSpec((m, RET_D), lambda b, t: (t, 0)),
            _const_spec((RET_HEADS, chunk, chunk)),
            _const_spec((RET_HEADS, chunk, RET_D)),
            _const_spec((RET_HEADS, chunk, RET_D)),
        ] + ([] if states is None else [state_spec(s) for s in state_shapes]),
        out_specs=[
            pl.BlockSpec((nb, tt, d), lambda b, t: (b, t, 0)),
            state_spec((1, w)),
            state_spec((CONV_W - 1, w)),
            state_spec((RET_HEADS, RET_D, RET_D)),
        ],
        out_shape=[
            jax.ShapeDtypeStruct((bsz, seq, d), F32),
            jax.ShapeDtypeStruct((bsz, 1, w), F32),
            jax.ShapeDtypeStruct((bsz, CONV_W - 1, w), F32),
            jax.ShapeDtypeStruct((bsz, RET_HEADS, RET_D, RET_D), F32),
        ],
        scratch_shapes=[pltpu.VMEM((nb, tt + CONV_PAD, w), F32)],
        compiler_params=pltpu.CompilerParams(
            dimension_semantics=("arbitrary", "arbitrary"), vmem_limit_bytes=VMEM_LIMIT_BYTES),
        name="even_mixer",
    )(x, mod4, ng, w_in, w_out, conv_w, conv_b, wg, gb, lam, gn, cos, sin, decay, q_dec, k_dec,
      *([] if states is None else states))


_GQ, _GK, _GV, _GR, _U, _OP, _SM = 0, 256, 512, 1024, 1536, 2048, 2560
_SM_I, _SM_F = GLA_RANK, GLA_RANK + MLSTM_HEADS


def _odd_kernel(x_ref, mod_ref, ng_ref, win_ref, wout_ref, wgate_ref, bgate_ref, ggn_ref,
                cw_ref, cb_ref, wqk_ref, wv_ref, bi_ref, bf_ref, mgn_ref, *rest, nb, tt, gchunk, mchunk, zero_state):
    if zero_state:
        y_ref, s_ref, c_ref, n_ref, m_ref, u_ref, cbuf = rest
    else:
        s0_ref, c0_ref, n0_ref, m0_ref, u0_ref, y_ref, s_ref, c_ref, n_ref, m_ref, u_ref, cbuf = rest
    d = D_MODEL
    m = nb * tt
    wd = MLSTM_HEADS * MLSTM_DH
    dh = MLSTM_DH

    @pl.when(pl.program_id(1) == 0)
    def _():
        cbuf[:, 0:CONV_PAD, :] = jnp.zeros((nb, CONV_PAD, wd), F32)
        if zero_state:
            for ref in (s_ref, c_ref, n_ref, m_ref):
                ref[...] = jnp.zeros(ref.shape, F32)
        else:
            s_ref[...] = s0_ref[...]
            c_ref[...] = c0_ref[...]
            n_ref[...] = n0_ref[...]
            m_ref[...] = m0_ref[...]
            cbuf[:, CONV_PAD - (CONV_W - 1):CONV_PAD, :] = u0_ref[...]

    x = x_ref[...]
    mod = mod_ref[...]
    hn = _rms_mod(x, ng_ref[...], mod[:, :, d:2 * d], mod[:, :, 0:d]).reshape(m, d).astype(BF16)

    proj = lambda lo, width: jnp.dot(hn, win_ref[:, lo:lo + width], preferred_element_type=F32)

    lane = lax.broadcasted_iota(jnp.int32, (1, 128), 1)

    def tri(n):
        row = lax.broadcasted_iota(jnp.int32, (n, n), 0)
        col = lax.broadcasted_iota(jnp.int32, (n, n), 1)
        return row >= col, row == col

    small = proj(_SM, 128)
    u = proj(_U, wd)
    uc, tail = _conv_taps(cbuf, u, cw_ref, cb_ref, nb, tt)
    u_ref[...] = tail
    uc = _silu(uc)
    gq = proj(_GQ, 256)
    gk = proj(_GK, 256) * (GLA_DK ** -0.5)
    log_alpha = _log_sigmoid(_dot(small, wgate_ref[...]) + bgate_ref[...]) / GLA_TAU
    cum = _seg_cumsum(log_alpha, gchunk)
    mq, mk, mv = [], [], []
    for h in range(MLSTM_HEADS):
        lo = h * dh
        qk = _dot(uc[:, lo:lo + dh], wqk_ref[h])
        mq.append(qk[:, 0:dh])
        mk.append(qk[:, dh:2 * dh] * (dh ** -0.5))
        mv.append(_dot(u[:, lo:lo + dh], wv_ref[h]))
    gv = proj(_GV, GLA_HEADS * GLA_DV)
    ic_all = small + bi_ref[...]
    fc_all = _log_sigmoid(small + bf_ref[...])
    b_all = _seg_cumsum(fc_all, mchunk)

    n_gc = tt // gchunk
    n_mc = tt // mchunk
    g_units = [(bi, c) for bi in range(nb) for c in range(n_gc)]
    m_units = [(bi, c) for bi in range(nb) for c in range(n_mc)]
    g_causal, _ = tri(gchunk)
    m_causal, m_diag = tri(mchunk)
    own = [(lane >= hh * GLA_DK) & (lane < (hh + 1) * GLA_DK) for hh in range(2)]

    mA = {}
    for (bi, c) in m_units:
        rows = slice(bi * tt + c * mchunk, bi * tt + (c + 1) * mchunk)
        for h in range(MLSTM_HEADS):
            b_col = b_all[rows, _SM_F + h:_SM_F + h + 1]
            i_col = ic_all[rows, _SM_I + h:_SM_I + h + 1]
            g_col = i_col - b_col
            g_row = jnp.sum(jnp.where(m_diag, g_col, 0.0), axis=0, keepdims=True)
            dlog = jnp.where(m_causal, b_col + g_row, -jnp.inf)
            mA[bi, c, h] = (b_col, i_col, dlog, jnp.max(dlog, axis=-1, keepdims=True),
                            _dot_nt(mq[h][rows], mk[h][rows]))

    gA = {}
    for (bi, c) in g_units:
        rows = slice(bi * tt + c * gchunk, bi * tt + (c + 1) * gchunk)
        cm = cum[rows]
        last = cm[gchunk - 1:gchunk, :]
        ref = 0.5 * last
        qc, kc = gq[rows], gk[rows]
        gA[bi, c] = (qc * jnp.exp(cm - ref), kc * jnp.exp(ref - cm), qc * jnp.exp(cm), kc * jnp.exp(last - cm),
                     jnp.exp(last))

    gB = {}
    for (bi, c) in g_units:
        rows = slice(bi * tt + c * gchunk, bi * tt + (c + 1) * gchunk)
        q_in, k_in, q_st, k_st, e_last = gA[bi, c]
        for h in range(GLA_HEADS):
            p, hh = divmod(h, 2)
            lanes = slice(p * 128, (p + 1) * 128)
            vh = gv[rows, h * GLA_DV:(h + 1) * GLA_DV]
            att = jnp.where(g_causal, _dot_nt(jnp.where(own[hh], q_in[:, lanes], 0.0), k_in[:, lanes]), 0.0)
            gB[bi, c, h] = (att, _dot_tn(vh, jnp.where(own[hh], k_st[:, lanes], 0.0)), vh)

    mB = {}
    for bi in range(nb):
        m_row = m_ref[bi]
        for c in range(n_mc):
            m_new_row = m_row
            for h in range(MLSTM_HEADS):
                b_col, i_col, dlog, rmax, qk = mA[bi, c, h]
                a_col = b_col + m_row[:, h:h + 1]
                m_t = jnp.maximum(a_col, rmax)
                m_last = m_t[mchunk - 1:mchunk, :]
                w_in = jnp.exp(a_col - m_t)
                w_last = jnp.exp((b_col[mchunk - 1:mchunk, :] - b_col) + i_col - m_last)
                mB[bi, c, h] = (qk * jnp.exp(dlog - m_t), w_in, w_last, jnp.exp(-m_t))
                m_new_row = jnp.where(lane == h, m_last, m_new_row)
            m_row = m_new_row
        m_ref[bi] = m_row

    gC = {}
    for (bi, c) in g_units:
        for h in range(GLA_HEADS):
            att, inc, vh = gB[bi, c, h]
            gC[bi, c, h] = _dot(att, vh)
    mC = {}
    for (bi, c) in m_units:
        rows = slice(bi * tt + c * mchunk, bi * tt + (c + 1) * mchunk)
        for h in range(MLSTM_HEADS):
            qkw, w_in, w_last, e_neg = mB[bi, c, h]
            kw = mk[h][rows] * w_last
            mC[bi, c, h] = (_dot(qkw, mv[h][rows]), jnp.sum(qkw, axis=-1, keepdims=True),
                            _dot_tn(kw, mv[h][rows]), jnp.sum(kw, axis=0, keepdims=True))

    gla_o = [[] for _ in range(GLA_HEADS)]
    ml_o = [[] for _ in range(MLSTM_HEADS)]
    for bi in range(nb):
        states = [s_ref[bi, p] for p in range(GLA_HEADS // 2)]
        for c in range(n_gc):
            q_st, e_last = gA[bi, c][2], gA[bi, c][4]
            for h in range(GLA_HEADS):
                p, hh = divmod(h, 2)
                lanes = slice(p * 128, (p + 1) * 128)
                gla_o[h].append(gC[bi, c, h] + _dot_nt(jnp.where(own[hh], q_st[:, lanes], 0.0), states[p]))
            for p in range(GLA_HEADS // 2):
                states[p] = (e_last[:, p * 128:(p + 1) * 128] * states[p] + gB[bi, c, 2 * p][1]) + gB[bi, c, 2 * p + 1][1]
        for p in range(GLA_HEADS // 2):
            s_ref[bi, p] = states[p]
        for h in range(MLSTM_HEADS):
            c_st = c_ref[bi, h]
            n_st = n_ref[bi, h:h + 1, :]
            for c in range(n_mc):
                rows = slice(bi * tt + c * mchunk, bi * tt + (c + 1) * mchunk)
                qkw, w_in, w_last, e_neg = mB[bi, c, h]
                num1, rsum, c_inc, n_inc = mC[bi, c, h]
                qh = mq[h][rows]
                num = num1 + w_in * _dot(qh, c_st)
                den = rsum + w_in * jnp.sum(qh * n_st, axis=-1, keepdims=True)
                ml_o[h].append(num / jnp.maximum(jnp.abs(den), e_neg))
                w_in_last = w_in[mchunk - 1:mchunk, :]
                c_st = w_in_last * c_st + c_inc
                n_st = w_in_last * n_st + n_inc
            c_ref[bi, h] = c_st
            n_ref[bi, h:h + 1, :] = n_st

    cat = lambda parts: parts[0] if len(parts) == 1 else jnp.concatenate(parts, axis=0)
    gr = proj(_GR, GLA_HEADS * GLA_DV)
    og = proj(_OP, wd)
    mix = []
    for h in range(GLA_HEADS):
        lo = h * GLA_DV
        mix.append((_silu(gr[:, lo:lo + GLA_DV]) * _head_norm(cat(gla_o[h]), ggn_ref[:, lo:lo + GLA_DV])).astype(BF16))
    for h in range(MLSTM_HEADS):
        lo = h * dh
        mix.append((_sigmoid(og[:, lo:lo + dh]) * _head_norm(cat(ml_o[h]), mgn_ref[:, lo:lo + dh])).astype(BF16))

    y = jnp.dot(jnp.concatenate(mix, axis=1), wout_ref[:, 0:D_MODEL], preferred_element_type=F32)
    y_ref[...] = x + mod[:, :, 2 * d:3 * d] * y.reshape(nb, tt, d)


def _odd_mixer_call(x, mod4, row0, ng, w_in, w_out, w_gate, b_gate, ggn, conv_w, conv_b, w_qk, w_v, b_i, b_f, mgn,
                    states):
    bsz, seq, d = x.shape
    nb, tt, chunk = _tiling(bsz, seq)
    m = nb * tt
    wd = MLSTM_HEADS * MLSTM_DH
    dh = MLSTM_DH
    state_spec = lambda shape: pl.BlockSpec((nb,) + shape, lambda b, t: (b,) + (0,) * len(shape))
    state_shapes = [(GLA_HEADS // 2, GLA_DV, 2 * GLA_DK), (MLSTM_HEADS, dh, dh), (MLSTM_HEADS, dh), (1, 128),
                    (CONV_W - 1, wd)]
    kern = functools.partial(_odd_kernel, nb=nb, tt=tt, gchunk=chunk, mchunk=min(2 * chunk, tt),
                             zero_state=states is None)
    return pl.pallas_call(
        kern,
        grid=(bsz // nb, seq // tt),
        in_specs=[
            pl.BlockSpec((nb, tt, d), lambda b, t: (b, t, 0)),
            _mod_spec(nb, 1, row0),
            _layer_spec((1, d), 1),
            _const_spec((d, ODD_IN_PADDED)),
            _const_spec((d, d + WEIGHT_COL_PAD)),
            _const_spec((128, GLA_HEADS * GLA_DK)),
            _const_spec((1, GLA_HEADS * GLA_DK)),
            _const_spec((1, GLA_HEADS * GLA_DV)),
            _const_spec((CONV_W, wd)),
            _const_spec((1, wd)),
            _const_spec((MLSTM_HEADS, dh, 2 * dh)),
            _const_spec((MLSTM_HEADS, dh, dh)),
            _const_spec((1, 128)),
            _const_spec((1, 128)),
            _const_spec((1, wd)),
        ] + ([] if states is None else [state_spec(s) for s in state_shapes]),
        out_specs=[pl.BlockSpec((nb, tt, d), lambda b, t: (b, t, 0))] + [state_spec(s) for s in state_shapes],
        out_shape=[jax.ShapeDtypeStruct((bsz, seq, d), F32)]
        + [jax.ShapeDtypeStruct((bsz,) + s, F32) for s in state_shapes],
        scratch_shapes=[pltpu.VMEM((nb, tt + CONV_PAD, wd), F32)],
        compiler_params=pltpu.CompilerParams(
            dimension_semantics=("arbitrary", "arbitrary"), vmem_limit_bytes=VMEM_LIMIT_BYTES),
        name="odd_mixer",
    )(x, mod4, ng, w_in, w_out, w_gate, b_gate, ggn, conv_w, conv_b, w_qk, w_v, b_i, b_f, mgn,
      *([] if states is None else states))


def _ffn_norm(x, mod, gain):
    d = D_MODEL
    rows = x.shape[0] * x.shape[1]
    return _rms_mod(x, gain, mod[:, :, 4 * d:5 * d], mod[:, :, 3 * d:4 * d]).reshape(rows, d).astype(BF16)


def _swiglu(load_hn, w1_ref, w2_ref, between=None):
    acc = None
    for j in range(D_FF // FFN_COLS):
        lo = j * FFN_COLS
        hn = load_hn()
        gate = jnp.dot(hn, w1_ref[:, lo:lo + FFN_COLS], preferred_element_type=F32)
        up = jnp.dot(hn, w1_ref[:, D_FF + lo:D_FF + lo + FFN_COLS], preferred_element_type=F32)
        act = (_silu(gate) * up).astype(BF16)
        part = jnp.dot(act, w2_ref[lo:lo + FFN_COLS, :], preferred_element_type=F32)
        acc = part if acc is None else acc + part
        if j == 0 and between is not None:
            between()
    return acc


def _ffn_finish(x, mod, acc, fg_ref, final):
    d = D_MODEL
    x2 = x + mod[:, :, 5 * d:6 * d] * acc.reshape(x.shape)
    if final:
        x2 = (x2 * lax.rsqrt(jnp.mean(x2 * x2, axis=-1, keepdims=True) + NORM_EPS)) * fg_ref[...][None]
    return x2


def _ffn_kernel(xp_ref, xn_ref, xs_ref, modp_ref, modn_ref, mods_ref, ng_ref, w1_ref, w2_ref, fg_ref,
                yp_ref, ys_ref, hn_scr, *, n_tiles, final):
    s = pl.program_id(0)

    @pl.when(s == 0)
    def _():
        hn_scr[0] = _ffn_norm(xp_ref[...], modp_ref[...], ng_ref[...])

    @pl.when(s < n_tiles)
    def _():
        slot = lax.rem(s, 2)

        def prepare_next():
            hn_scr[1 - slot] = _ffn_norm(xn_ref[...], modn_ref[...], ng_ref[...])

        acc = _swiglu(lambda: hn_scr[slot], w1_ref, w2_ref, prepare_next)
        yp_ref[...] = _ffn_finish(xp_ref[...], modp_ref[...], acc, fg_ref, final)

    @pl.when(s == n_tiles)
    def _():
        xs = xs_ref[...]
        mods = mods_ref[...]
        hn_s = _ffn_norm(xs, mods, ng_ref[...])
        acc = _swiglu(lambda: hn_s, w1_ref, w2_ref)
        ys_ref[...] = _ffn_finish(xs, mods, acc, fg_ref, final)


FFN_ROWS = 512


def _ffn_call(xp, xs, mod4, rowp, rows_, ng, w1, w2, fg, layer, final):
    bp, tp, d = xp.shape
    bs, ts, _ = xs.shape
    per_b = tp // FFN_ROWS
    n_tiles = bp * per_b

    def tile(shift):
        def index(s):
            c = jnp.minimum(s + shift, n_tiles - 1)
            return c // per_b, c % per_b, 0
        return pl.BlockSpec((1, FFN_ROWS, d), index)

    def mod_p(shift):
        def index(s):
            c = jnp.minimum(s + shift, n_tiles - 1)
            return layer, rowp + c // per_b, 0, 0
        return pl.BlockSpec((None, 1, 1, 6 * d), index)

    const = lambda shape: pl.BlockSpec(shape, lambda s: (0,) * len(shape))
    layer_slab = lambda shape: pl.BlockSpec((None,) + shape, lambda s: (layer,) + (0,) * len(shape))
    kern = functools.partial(_ffn_kernel, n_tiles=n_tiles, final=final)
    return pl.pallas_call(
        kern,
        grid=(n_tiles + 1,),
        in_specs=[
            tile(0), tile(1), const((bs, ts, d)),
            mod_p(0), mod_p(1),
            pl.BlockSpec((None, bs, 1, 6 * d), lambda s: (layer, rows_ // bs, 0, 0)),
            layer_slab((1, d)), layer_slab((d, 2 * D_FF)), layer_slab((D_FF, d)), const((1, d)),
        ],
        out_specs=[tile(0), const((bs, ts, d))],
        out_shape=[jax.ShapeDtypeStruct((bp, tp, d), F32), jax.ShapeDtypeStruct((bs, ts, d), F32)],
        scratch_shapes=[pltpu.VMEM((2, FFN_ROWS, d), BF16)],
        compiler_params=pltpu.CompilerParams(
            dimension_semantics=("arbitrary",), vmem_limit_bytes=VMEM_LIMIT_BYTES),
        name="swiglu_ffn",
    )(xp, xp, xs, mod4, mod4, mod4, ng, w1, w2, fg)


def _block_diag(blocks):
    n, r, c = blocks.shape
    eye = jnp.eye(n, dtype=blocks.dtype)
    return (eye[:, None, :, None] * blocks[:, :, None, :]).reshape(n * r, n * c)


def _lru_gate_layout(gate_w, gate_b):
    halves = []
    per = LRU_HEADS // 2
    for j in range(2):
        blk = gate_w[j * per:(j + 1) * per]
        halves.append(jnp.concatenate([_block_diag(blk[:, :, :LRU_BLOCK]), _block_diag(blk[:, :, LRU_BLOCK:])], axis=1))
    bias = jnp.stack([gate_b[:, :LRU_BLOCK].reshape(-1), gate_b[:, LRU_BLOCK:].reshape(-1)])
    return jnp.stack(halves).astype(BF16), bias


def _odd_in_layout(w):
    pad = jnp.zeros((w.shape[0], 128 - GLA_RANK - 2 * MLSTM_HEADS), w.dtype)
    return jnp.concatenate([w[:, 0:1024], w[:, 1040:2576], w[:, 1024:1040], w[:, 2576:2584], pad], axis=1).astype(BF16)


def _pad_cols(w):
    return jnp.pad(w, ((0, 0), (0, WEIGHT_COL_PAD))).astype(BF16)


def _lane_row(vec, offset):
    return jnp.zeros((1, 128), F32).at[0, offset:offset + vec.shape[0]].set(vec)


def _gla_state_in(s):
    b = s.shape[0]
    return jnp.swapaxes(s.reshape(b, GLA_HEADS // 2, 2 * GLA_DK, GLA_DV), -1, -2)


def _gla_state_out(s):
    b = s.shape[0]
    return jnp.swapaxes(s, -1, -2).reshape(b, GLA_HEADS, GLA_DK, GLA_DV)


def _even_layer(x, mod4, row0, states, P, pos0):
    bsz = x.shape[0]
    even_states = None
    if states is not None:
        lru_h, lru_conv, ret = states
        even_states = (lru_h.reshape(bsz, 1, LRU_WIDTH), lru_conv, ret)
    x, h_new, conv_new, ret_new = _even_mixer_call(
        x, mod4, row0, P['norm_mix_g'], P['a_w_in'], P['a_w_out'], P['lru_conv_w'], P['lru_conv_b'],
        P['lru_wg'], P['lru_gb'], P['lru_lambda'], P['ret_gn_g'], pos0, even_states)
    return x, (h_new.reshape(1, bsz, LRU_WIDTH), conv_new[None], ret_new[None])


def _odd_layer(x, mod4, row0, states, P):
    bsz = x.shape[0]
    odd_states = None
    if states is not None:
        gla, ml_c, ml_n, ml_m, ml_conv = states
        m_in = jnp.pad(ml_m, ((0, 0), (0, 128 - MLSTM_HEADS))).reshape(bsz, 1, 128)
        odd_states = (_gla_state_in(gla), ml_c, ml_n, m_in, ml_conv)
    x, gla_new, c_new, n_new, m_new, mconv_new = _odd_mixer_call(
        x, mod4, row0, P['norm_mix_g'], P['c_w_in'], P['c_w_out'], P['gla_w_gate'], P['gla_b_gate'],
        P['gla_gn_g'], P['mlstm_conv_w'], P['mlstm_conv_b'], P['mlstm_w_qk'], P['mlstm_w_v'],
        P['mlstm_b_i'], P['mlstm_b_f'], P['mlstm_gn_g'], odd_states)
    return x, (_gla_state_out(gla_new)[None], c_new[None], n_new[None], m_new[:, 0, :MLSTM_HEADS][None],
               mconv_new[None])


def kernel(x_prompt, x_sample, c_prompt, c_sample, state_lru_h, state_lru_conv, state_ret, state_gla, state_mlstm_C, state_mlstm_n, state_mlstm_m, state_mlstm_conv, w_mod, b_mod, norm_mix_g, norm_ffn_g, w_ffn_in, w_ffn_out, final_norm_g, a_w_in, a_w_out, lru_conv_w, lru_conv_b, lru_gate_w, lru_gate_b, lru_lambda, ret_gn_g, c_w_in, c_w_out, gla_w_gate, gla_b_gate, gla_gn_g, mlstm_conv_w, mlstm_conv_b, mlstm_w_qk, mlstm_w_v, mlstm_b_if, mlstm_gn_g):
    bp = x_prompt.shape[0]
    bs = x_sample.shape[0]
    lru_wg, lru_gb = _lru_gate_layout(lru_gate_w[0], lru_gate_b[0])
    w_gate_pad = jnp.zeros((128, GLA_HEADS * GLA_DK), F32).at[0:GLA_RANK].set(gla_w_gate[0]).astype(BF16)
    P = {
        'norm_mix_g': norm_mix_g.reshape(-1, 1, D_MODEL), 'norm_ffn_g': norm_ffn_g.reshape(-1, 1, D_MODEL),
        'final_norm_g': final_norm_g.reshape(1, D_MODEL),
        'w_ffn_in': w_ffn_in.astype(BF16), 'w_ffn_out': w_ffn_out.astype(BF16),
        'a_w_in': _pad_cols(a_w_in[0]), 'a_w_out': _pad_cols(a_w_out[0]),
        'lru_conv_w': lru_conv_w[0], 'lru_conv_b': lru_conv_b, 'lru_wg': lru_wg, 'lru_gb': lru_gb,
        'lru_lambda': lru_lambda, 'ret_gn_g': ret_gn_g,
        'c_w_in': _odd_in_layout(c_w_in[0]), 'c_w_out': _pad_cols(c_w_out[0]),
        'gla_w_gate': w_gate_pad, 'gla_b_gate': gla_b_gate, 'gla_gn_g': gla_gn_g,
        'mlstm_conv_w': mlstm_conv_w[0], 'mlstm_conv_b': mlstm_conv_b,
        'mlstm_w_qk': mlstm_w_qk[0].astype(BF16), 'mlstm_w_v': mlstm_w_v[0].astype(BF16),
        'mlstm_b_i': _lane_row(mlstm_b_if[0, :MLSTM_HEADS], _SM_I),
        'mlstm_b_f': _lane_row(mlstm_b_if[0, MLSTM_HEADS:], _SM_F),
        'mlstm_gn_g': mlstm_gn_g,
    }
    mod = _modulation(jnp.concatenate([c_sample, c_prompt], axis=0), w_mod, b_mod)
    mod4 = mod.reshape(mod.shape[0], bs + bp, 1, 6 * D_MODEL)
    ffn = lambda xp, xs, layer, final: _ffn_call(xp, xs, mod4, bs, 0, P['norm_ffn_g'], P['w_ffn_in'], P['w_ffn_out'],
                                                 P['final_norm_g'], layer, final)
    xp, sp_even = _even_layer(x_prompt, mod4, bs, None, P, 0)
    xs, ss_even = _even_layer(x_sample, mod4, 0, (state_lru_h[0], state_lru_conv[0], state_ret[0]), P, PAST_LEN)
    xp, xs = ffn(xp, xs, 0, False)
    xp, sp_odd = _odd_layer(xp, mod4, bs, None, P)
    xs, ss_odd = _odd_layer(xs, mod4, 0, (state_gla[0], state_mlstm_C[0], state_mlstm_n[0], state_mlstm_m[0],
                                          state_mlstm_conv[0]), P)
    y_p, y_s = ffn(xp, xs, 1, True)
    return (y_p, y_s) + sp_even + sp_odd + ss_even + ss_odd
```
